```python
import jax
import jax.numpy as jnp
from jax import lax
import numpy as np

D_MODEL = 1024
BATCH = 16
SEQ = 2048
DEPTH = 4

N_MIXERS = 3
HEAD_DIM = 64
MEM_LEN = 256
MEM_HEADS = 4
MEM_WIDTH = MEM_HEADS * HEAD_DIM
MIX_WIDTH = 3 * D_MODEL // 4
CONV_WIDTH = 3
DIL_GROUPS = ((128, 1), (512, 4), (2048, 16))
DIL_HEADS = 4
DIL_WIDTH = DIL_HEADS * HEAD_DIM
DIL_QKV_WIDTH = len(DIL_GROUPS) * DIL_WIDTH
POOL_WINDOWS = (2, 4, 8, 16)
POOL_GROUP = MIX_WIDTH // len(POOL_WINDOWS)
D_FF = 2816
N_EXPERTS = 8
TOP_K = 2
D_FF_EXPERT = 3584
MOE_BLOCK = 256
LN_EPS = 1e-5
NEG_INF = -1e30
DEEPNORM_ALPHA = (2 * DEPTH) ** 0.25
DEEPNORM_BETA = (8 * DEPTH) ** -0.25
N_A = (DEPTH + 2) // 3
N_B = (DEPTH + 1) // 3
N_C = DEPTH // 3
N_DENSE = (DEPTH + 1) // 2
N_MOE = DEPTH // 2

kernel_name = "hybrid_conv_dilattn_pool_moe_encoder"


def _normal(key, shape, fan_in, scale=1.0):
    return jax.random.normal(key, shape, jnp.float32) * (scale * fan_in ** -0.5)


def setup_inputs(seed: int = 0) -> dict:
    key = jax.random.key(seed)
    ks = iter(jax.random.split(key, 32))
    d = D_MODEL
    beta = DEEPNORM_BETA
    x = jax.random.normal(next(ks), (BATCH, SEQ, d), jnp.float32)
    mem = jax.random.normal(next(ks), (BATCH, MEM_LEN, d), jnp.float32)
    w_mem_kv = jnp.concatenate([_normal(next(ks), (d, MEM_WIDTH), d),
                                _normal(next(ks), (d, MEM_WIDTH), d, beta)], axis=1)
    a_w_in = _normal(next(ks), (N_A, d, 3 * MIX_WIDTH + MEM_WIDTH), d)
    a_conv_w = _normal(next(ks), (N_A, CONV_WIDTH, MIX_WIDTH), CONV_WIDTH)
    a_w_out = _normal(next(ks), (N_A, MIX_WIDTH + MEM_WIDTH, d), MIX_WIDTH + MEM_WIDTH, beta)
    b_w_in = jnp.concatenate([_normal(next(ks), (N_B, d, 2 * DIL_QKV_WIDTH), d),
                              _normal(next(ks), (N_B, d, DIL_QKV_WIDTH), d, beta),
                              _normal(next(ks), (N_B, d, MEM_WIDTH), d)], axis=-1)
    b_w_out = _normal(next(ks), (N_B, DIL_WIDTH + MEM_WIDTH, d), DIL_WIDTH + MEM_WIDTH, beta)
    c_w_in = _normal(next(ks), (N_C, d, MIX_WIDTH + MEM_WIDTH), d)
    c_pool_w = _normal(next(ks), (N_C, len(POOL_WINDOWS), POOL_GROUP, POOL_GROUP), POOL_GROUP)
    c_pool_scale = 1.0 + 0.1 * jax.random.normal(next(ks), (N_C, MIX_WIDTH), jnp.float32)
    c_w_out = _normal(next(ks), (N_C, MIX_WIDTH + MEM_WIDTH, d), MIX_WIDTH + MEM_WIDTH, beta)
    ln_g = 1.0 + 0.02 * jax.random.normal(next(ks), (DEPTH, 2, d), jnp.float32)
    ln_b = 0.02 * jax.random.normal(next(ks), (DEPTH, 2, d), jnp.float32)
    ffn_w_gate = _normal(next(ks), (N_DENSE, d, D_FF), d)
    ffn_w_up = _normal(next(ks), (N_DENSE, d, D_FF), d)
    ffn_w_down = _normal(next(ks), (N_DENSE, D_FF, d), D_FF, beta)
    moe_router = _normal(next(ks), (N_MOE, d, N_EXPERTS), d)
    moe_w_gate = _normal(next(ks), (N_MOE, N_EXPERTS, d, D_FF_EXPERT), d)
    moe_w_up = _normal(next(ks), (N_MOE, N_EXPERTS, d, D_FF_EXPERT), d)
    moe_w_down = _normal(next(ks), (N_MOE, N_EXPERTS, D_FF_EXPERT, d), D_FF_EXPERT, beta)
    return {"x": x, "mem": mem, "w_mem_kv": w_mem_kv,
            "a_w_in": a_w_in, "a_conv_w": a_conv_w, "a_w_out": a_w_out,
            "b_w_in": b_w_in, "b_w_out": b_w_out,
            "c_w_in": c_w_in, "c_pool_w": c_pool_w, "c_pool_scale": c_pool_scale, "c_w_out": c_w_out,
            "ln_g": ln_g, "ln_b": ln_b,
            "ffn_w_gate": ffn_w_gate, "ffn_w_up": ffn_w_up, "ffn_w_down": ffn_w_down,
            "moe_router": moe_router, "moe_w_gate": moe_w_gate, "moe_w_up": moe_w_up,
            "moe_w_down": moe_w_down}


def _layer_norm(x, g, b):
    xf = x.astype(jnp.float32)
    mu = jnp.mean(xf, axis=-1, keepdims=True)
    var = jnp.mean(jnp.square(xf - mu), axis=-1, keepdims=True)
    return ((xf - mu) * lax.rsqrt(var + LN_EPS) * g + b).astype(x.dtype)


def _alibi_slopes(n):
    return 2.0 ** (-8.0 * jnp.arange(1, n + 1, dtype=jnp.float32) / n)


def _mem_attention(q, mem_k, mem_v):
    b, s, _ = q.shape
    q = q.reshape(b, s, MEM_HEADS, HEAD_DIM)
    scores = jnp.einsum('bshc,bmhc->bhsm', q, mem_k).astype(jnp.float32) * HEAD_DIM ** -0.5
    p = jax.nn.softmax(scores, axis=-1)
    out = jnp.einsum('bhsm,bmhc->bshc', p.astype(mem_v.dtype), mem_v)
    return out.reshape(b, s, MEM_WIDTH)


def _short_conv_mixer(h, conv_w):
    gate_b, gate_c, u = jnp.split(h, 3, axis=-1)
    z = jnp.pad(gate_c * u, ((0, 0), (1, 1), (0, 0)))
    conv = conv_w[0] * z[:, :-2] + conv_w[1] * z[:, 1:-1] + conv_w[2] * z[:, 2:]
    return gate_b * conv


def _dilated_group(q, k, v, window, dilation, slopes):
    b, s, h, c = q.shape
    radius = (window // 2) // dilation
    blk = radius
    n_sub = s // dilation
    nb = -(-n_sub // blk)
    lp = nb * blk

    def by_residue(a):
        return a.reshape(b, n_sub, dilation, h, c)

    qs = jnp.pad(by_residue(q), ((0, 0), (0, lp - n_sub), (0, 0), (0, 0), (0, 0)))
    qs = qs.reshape(b, nb, blk, dilation, h, c)

    def key_windows(a):
        ap = jnp.pad(by_residue(a), ((0, 0), (blk, lp - n_sub + blk), (0, 0), (0, 0), (0, 0)))
        ap = ap.reshape(b, nb + 2, blk, dilation, h, c)
        return jnp.concatenate([ap[:, :-2], ap[:, 1:-1], ap[:, 2:]], axis=2)

    kw = key_windows(k)
    vw = key_windows(v)
    scores = jnp.einsum('bnqrhc,bnkrhc->bnrhqk', qs, kw).astype(jnp.float32) * c ** -0.5
    qi = jnp.arange(blk)
    ki = jnp.arange(3 * blk)
    diff = ki[None, :] - blk - qi[:, None]
    key_j = jnp.arange(nb)[:, None] * blk - blk + ki[None, :]
    mask = (jnp.abs(diff) <= radius)[None] & ((key_j >= 0) & (key_j < n_sub))[:, None, :]
    bias = -slopes[:, None, None] * (jnp.abs(diff) * dilation).astype(jnp.float32)[None]
    scores = jnp.where(mask[None, :, None, None], scores + bias[None, None, None], NEG_INF)
    m = jnp.max(scores, axis=-1, keepdims=True)
    p = jnp.exp(scores - m)
    den = jnp.sum(p, axis=-1)
    lse = m[..., 0] + jnp.log(den)
    out = jnp.einsum('bnrhqk,bnkrhc->bnqrhc', p, vw.astype(jnp.float32))
    out = out / jnp.transpose(den, (0, 1, 4, 2, 3))[..., None]
    out = out.reshape(b, lp, dilation, h, c)[:, :n_sub].reshape(b, s, h, c)
    lse = jnp.transpose(lse, (0, 1, 4, 2, 3)).reshape(b, lp, dilation, h)[:, :n_sub].reshape(b, s, h)
    return out, lse


def _dilated_mixer(h):
    b, s, _ = h.shape
    n_g = len(DIL_GROUPS)
    q, k, v = [h[..., i * DIL_QKV_WIDTH:(i + 1) * DIL_QKV_WIDTH].reshape(b, s, n_g, DIL_HEADS, HEAD_DIM)
               for i in range(3)]
    slopes = _alibi_slopes(n_g * DIL_HEADS).reshape(n_g, DIL_HEADS)
    outs, lses = [], []
    for g, (window, dilation) in enumerate(DIL_GROUPS):
        o, l = _dilated_group(q[:, :, g], k[:, :, g], v[:, :, g], window, dilation, slopes[g])
        outs.append(o)
        lses.append(l)
    wts = jax.nn.softmax(jnp.stack(lses), axis=0)
    out = jnp.sum(wts[..., None] * jnp.stack(outs), axis=0)
    return out.reshape(b, s, DIL_WIDTH).astype(h.dtype)


def _pool_mixer(u, pool_w, pool_scale):
    b, s, _ = u.shape
    cs = jnp.pad(jnp.cumsum(u.astype(jnp.float32), axis=1), ((0, 0), (1, 0), (0, 0)))
    t = jnp.arange(s)
    pooled = []
    for g, w in enumerate(POOL_WINDOWS):
        lo = jnp.clip(t - w // 2, 0, s - 1)
        hi = jnp.clip(t + w // 2 - 1, 0, s - 1)
        csg = cs[..., g * POOL_GROUP:(g + 1) * POOL_GROUP]
        cnt = (hi - lo + 1).astype(jnp.float32)[None, :, None]
        pooled.append((jnp.take(csg, hi + 1, axis=1) - jnp.take(csg, lo, axis=1)) / cnt)
    pooled = jnp.stack(pooled, axis=2)
    ug = u.reshape(b, s, len(POOL_WINDOWS), POOL_GROUP)
    diff = (pooled - ug.astype(jnp.float32)).astype(u.dtype)
    y = jnp.einsum('bsgc,gcd->bsgd', diff, pool_w).reshape(b, s, MIX_WIDTH)
    return y * pool_scale


def _swiglu(x, w_gate, w_up, w_down):
    hdn = jax.nn.silu(jnp.einsum('bsd,df->bsf', x, w_gate)) * jnp.einsum('bsd,df->bsf', x, w_up)
    return jnp.einsum('bsf,fd->bsd', hdn, w_down)


def _moe(x, router, w_gate, w_up, w_down):
    b, s, d = x.shape
    xf = x.reshape(-1, d)
    t = xf.shape[0]
    logits = (xf @ router).astype(jnp.float32)
    top_logit, top_idx = lax.top_k(logits, TOP_K)
    top_w = jax.nn.softmax(top_logit, axis=-1)
    a = t * TOP_K
    e_flat = top_idx.reshape(a)
    tok_flat = jnp.repeat(jnp.arange(t, dtype=jnp.int32), TOP_K)
    w_flat = top_w.reshape(a)
    order = jnp.argsort(e_flat)
    e_sorted = e_flat[order]
    counts = jnp.bincount(e_flat, length=N_EXPERTS)
    padded = (counts + MOE_BLOCK - 1) // MOE_BLOCK * MOE_BLOCK
    pad_end = jnp.cumsum(padded)
    pad_start = pad_end - padded
    start = jnp.cumsum(counts) - counts
    dest = pad_start[e_sorted] + jnp.arange(a) - start[e_sorted]
    n_blocks = -(-a // MOE_BLOCK) + N_EXPERTS
    rows = n_blocks * MOE_BLOCK
    row_tok = jnp.zeros((rows,), jnp.int32).at[dest].set(tok_flat[order])
    row_w = jnp.zeros((rows,), jnp.float32).at[dest].set(w_flat[order])
    block_e = jnp.minimum(jnp.searchsorted(pad_end, jnp.arange(n_blocks) * MOE_BLOCK, side='right'),
                          N_EXPERTS - 1)
    xb = xf[row_tok].reshape(n_blocks, MOE_BLOCK, d)

    def expert_block(args):
        xblk, e = args
        hdn = jax.nn.silu(xblk @ w_gate[e]) * (xblk @ w_up[e])
        return hdn @ w_down[e]

    yb = lax.map(expert_block, (xb, block_e)).reshape(rows, d)
    y = jax.ops.segment_sum(yb * row_w[:, None].astype(yb.dtype), row_tok, num_segments=t)
    return y.reshape(b, s, d)


def reference(x, mem, w_mem_kv, a_w_in, a_conv_w, a_w_out, b_w_in, b_w_out,
              c_w_in, c_pool_w, c_pool_scale, c_w_out, ln_g, ln_b,
              ffn_w_gate, ffn_w_up, ffn_w_down, moe_router, moe_w_gate, moe_w_up, moe_w_down):
    b = x.shape[0]
    m_len = mem.shape[1]
    kv = jnp.einsum('bmd,de->bme', mem, w_mem_kv).reshape(b, m_len, 2, MEM_HEADS, HEAD_DIM)
    mem_k, mem_v = kv[:, :, 0], kv[:, :, 1]
    for i in range(DEPTH):
        kind = i % N_MIXERS
        j = i // N_MIXERS
        if kind == 0:
            h = jnp.einsum('bsd,de->bse', x, a_w_in[j])
            mix = _short_conv_mixer(h[..., :3 * MIX_WIDTH], a_conv_w[j])
            q_mem = h[..., 3 * MIX_WIDTH:]
            w_out = a_w_out[j]
        elif kind == 1:
            h = jnp.einsum('bsd,de->bse', x, b_w_in[j])
            mix = _dilated_mixer(h[..., :3 * DIL_QKV_WIDTH])
            q_mem = h[..., 3 * DIL_QKV_WIDTH:]
            w_out = b_w_out[j]
        else:
            h = jnp.einsum('bsd,de->bse', x, c_w_in[j])
            mix = _pool_mixer(h[..., :MIX_WIDTH], c_pool_w[j], c_pool_scale[j])
            q_mem = h[..., MIX_WIDTH:]
            w_out = c_w_out[j]
        mem_out = _mem_attention(q_mem, mem_k, mem_v)
        y = jnp.einsum('bse,ed->bsd', jnp.concatenate([mix, mem_out], axis=-1), w_out)
        x = _layer_norm(DEEPNORM_ALPHA * x + y, ln_g[i, 0], ln_b[i, 0])
        f = i // 2
        if i % 2 == 0:
            y = _swiglu(x, ffn_w_gate[f], ffn_w_up[f], ffn_w_down[f])
        else:
            y = _moe(x, moe_router[f], moe_w_gate[f], moe_w_up[f], moe_w_down[f])
        x = _layer_norm(DEEPNORM_ALPHA * x + y, ln_g[i, 1], ln_b[i, 1])
    return x
```

```python
import functools

import jax
import jax.numpy as jnp
from jax import lax
from jax.experimental import pallas as pl
from jax.experimental.pallas import tpu as pltpu

F32 = jnp.float32
BF16 = jnp.bfloat16

D_MODEL = 1024
DEPTH = 4
HEAD_DIM = 64
MEM_HEADS = 4
MEM_WIDTH = MEM_HEADS * HEAD_DIM
MIX_WIDTH = 3 * D_MODEL // 4
DIL_GROUPS = ((128, 1), (512, 4), (2048, 16))
DIL_HEADS = 4
DIL_WIDTH = DIL_HEADS * HEAD_DIM
DIL_RADIUS = 64
POOL_WINDOWS = (2, 4, 8, 16)
POOL_GROUP = MIX_WIDTH // len(POOL_WINDOWS)
N_EXPERTS = 8
TOP_K = 2
LN_EPS = 1e-5
NEG_INF = -1e30
ALPHA = (2 * DEPTH) ** 0.25

LANES = 128
HALO = 8
SEQ_TILE = 512
ROW_TILE = 512
EXPERT_BLOCK = 512
EXPERT_F_TILES = 2
Q_BLOCK = 128
VMEM_LIMIT = 56 * 1024 * 1024


def _params(*sem):
    return pltpu.CompilerParams(dimension_semantics=sem, vmem_limit_bytes=VMEM_LIMIT)


def _layer_norm(v, g, b):
    mu = jnp.mean(v, axis=-1, keepdims=True)
    d = v - mu
    var = jnp.mean(d * d, axis=-1, keepdims=True)
    return d * lax.rsqrt(var + LN_EPS) * g + b


def _dot(a, b):
    return jnp.dot(a, b, preferred_element_type=F32)


def _head_masks(width):
    col = lax.broadcasted_iota(jnp.int32, (1, width), 1)
    return [(col >= h * HEAD_DIM) & (col < (h + 1) * HEAD_DIM) for h in range(width // HEAD_DIM)]


def _stack_heads(q, masks):
    return jnp.concatenate([jnp.where(m, q, 0.0) for m in masks], axis=0).astype(BF16)


def _unstack_heads(o, masks, n):
    out = o[(len(masks) - 1) * n:]
    for h in range(len(masks) - 2, -1, -1):
        out = jnp.where(masks[h], o[h * n:(h + 1) * n], out)
    return out


def _mem_attention(q, kt, v):
    n = q.shape[0]
    masks = _head_masks(MEM_WIDTH)
    sc = _dot(_stack_heads(q, masks), kt) * HEAD_DIM ** -0.5
    p = jnp.exp(sc - jnp.max(sc, axis=-1, keepdims=True))
    p = p / jnp.sum(p, axis=-1, keepdims=True)
    return _unstack_heads(_dot(p.astype(BF16), v), masks, n)


def _mem_kv_body(mem_ref, w_ref, kt_ref, v_ref):
    kv = _dot(mem_ref[...].astype(BF16), w_ref[...])
    kt_ref[...] = kv[:, :MEM_WIDTH].T.astype(BF16)
    v_ref[...] = kv[:, MEM_WIDTH:].astype(BF16)


def _mem_kv(mem, w_kv):
    b, m, d = mem.shape
    return pl.pallas_call(
        _mem_kv_body,
        grid=(b,),
        in_specs=[pl.BlockSpec((None, m, d), lambda i: (i, 0, 0)),
                  pl.BlockSpec((d, 2 * MEM_WIDTH), lambda i: (0, 0))],
        out_specs=[pl.BlockSpec((None, MEM_WIDTH, m), lambda i: (i, 0, 0)),
                   pl.BlockSpec((None, m, MEM_WIDTH), lambda i: (i, 0, 0))],
        out_shape=[jax.ShapeDtypeStruct((b, MEM_WIDTH, m), BF16),
                   jax.ShapeDtypeStruct((b, m, MEM_WIDTH), BF16)],
        compiler_params=_params("arbitrary"),
        name="mem_kv",
    )(mem, w_kv)


def _ext_rows(xp_ref, xm_ref, xn_ref, seq_len):
    s = pl.program_id(1)
    ts = xm_ref.shape[0]
    xm = xm_ref[...]
    xe = jnp.concatenate([xp_ref[...], xm, xn_ref[...]], axis=0).astype(BF16)
    pos = s * ts - HALO + lax.broadcasted_iota(jnp.int32, (ts + 2 * HALO, 1), 0)
    valid = (pos >= 0) & (pos < seq_len)
    return xm, xe, valid


def _shift_rows(a, k):
    n = a.shape[0]
    return pltpu.roll(a, k % n, 0)


def _mixer_tail(mix, q_mem, xm, kt_ref, v_ref, wout_ref, g_ref, b_ref, o_ref):
    mem_out = _mem_attention(q_mem, kt_ref[...], v_ref[...])
    y = (_dot(mix.astype(BF16), wout_ref[:MIX_WIDTH, :])
         + _dot(mem_out.astype(BF16), wout_ref[MIX_WIDTH:, :]))
    o_ref[...] = _layer_norm(ALPHA * xm + y, g_ref[...], b_ref[...])


def _conv_layer_body(seq_len, xp_ref, xm_ref, xn_ref, win_ref, cw_ref, kt_ref, v_ref,
                     wout_ref, g_ref, b_ref, o_ref):
    ts = xm_ref.shape[0]
    xm, xe, valid = _ext_rows(xp_ref, xm_ref, xn_ref, seq_len)
    xmb = xm.astype(BF16)
    gate_b = _dot(xmb, win_ref[:, :MIX_WIDTH])
    cu = _dot(xe, win_ref[:, MIX_WIDTH:3 * MIX_WIDTH])
    q_mem = _dot(xmb, win_ref[:, 3 * MIX_WIDTH:])
    z = jnp.where(valid, cu[:, :MIX_WIDTH] * cu[:, MIX_WIDTH:], 0.0)
    cw = cw_ref[...]
    conv = (cw[0:1] * _shift_rows(z, 1)[HALO:HALO + ts]
            + cw[1:2] * z[HALO:HALO + ts]
            + cw[2:3] * _shift_rows(z, -1)[HALO:HALO + ts])
    _mixer_tail(gate_b * conv, q_mem, xm, kt_ref, v_ref, wout_ref, g_ref, b_ref, o_ref)


def _pool_layer_body(seq_len, xp_ref, xm_ref, xn_ref, win_ref, pw_ref, ps_ref, kt_ref, v_ref,
                     wout_ref, g_ref, b_ref, o_ref):
    ts = xm_ref.shape[0]
    s = pl.program_id(1)
    xm, xe, valid = _ext_rows(xp_ref, xm_ref, xn_ref, seq_len)
    u = jnp.where(valid, _dot(xe, win_ref[:, :MIX_WIDTH]), 0.0)
    q_mem = _dot(xm.astype(BF16), win_ref[:, MIX_WIDTH:])
    a2 = u + _shift_rows(u, 1)
    a4 = _shift_rows(a2, 1) + _shift_rows(a2, -1)
    a8 = _shift_rows(a4, 2) + _shift_rows(a4, -2)
    a16 = _shift_rows(a8, 4) + _shift_rows(a8, -4)
    col = lax.broadcasted_iota(jnp.int32, (1, MIX_WIDTH), 1)
    pos = s * ts + lax.broadcasted_iota(jnp.int32, (ts, 1), 0)
    num = a16[HALO:HALO + ts]
    cnt = None
    for gi in range(len(POOL_WINDOWS) - 1, -1, -1):
        w = POOL_WINDOWS[gi]
        c_w = (jnp.minimum(pos + (w // 2 - 1), seq_len - 1) - jnp.maximum(pos - w // 2, 0) + 1).astype(F32)
        if cnt is None:
            cnt = jnp.broadcast_to(c_w, (ts, MIX_WIDTH))
        else:
            in_group = col < (gi + 1) * POOL_GROUP
            num = jnp.where(in_group, (a2, a4, a8)[gi][HALO:HALO + ts], num)
            cnt = jnp.where(in_group, c_w, cnt)
    diff = num / cnt - u[HALO:HALO + ts]
    mix = _dot(diff.astype(BF16), pw_ref[...]) * ps_ref[...]
    _mixer_tail(mix, q_mem, xm, kt_ref, v_ref, wout_ref, g_ref, b_ref, o_ref)


def _mixer_layer(body, x, w_in, extra, kt, v, w_out, g, b):
    bsz, seq_len, d = x.shape
    ts = SEQ_TILE
    n_halo_blocks = seq_len // HALO
    const2 = lambda i, s: (0, 0)
    in_specs = [
        pl.BlockSpec((None, HALO, d), lambda i, s: (i, jnp.maximum(s * (ts // HALO) - 1, 0), 0)),
        pl.BlockSpec((None, ts, d), lambda i, s: (i, s, 0)),
        pl.BlockSpec((None, HALO, d), lambda i, s: (i, jnp.minimum((s + 1) * (ts // HALO), n_halo_blocks - 1), 0)),
        pl.BlockSpec(w_in.shape, const2),
    ]
    in_specs += [pl.BlockSpec(e.shape, const2) for e in extra]
    in_specs += [
        pl.BlockSpec((None,) + kt.shape[1:], lambda i, s: (i, 0, 0)),
        pl.BlockSpec((None,) + v.shape[1:], lambda i, s: (i, 0, 0)),
        pl.BlockSpec(w_out.shape, const2),
        pl.BlockSpec((1, d), const2),
        pl.BlockSpec((1, d), const2),
    ]
    return pl.pallas_call(
        functools.partial(body, seq_len),
        grid=(bsz, seq_len // ts),
        in_specs=in_specs,
        out_specs=pl.BlockSpec((None, ts, d), lambda i, s: (i, s, 0)),
        out_shape=jax.ShapeDtypeStruct(x.shape, F32),
        compiler_params=_params("arbitrary", "arbitrary"),
        name=body.__name__.strip("_"),
    )(x, x, x, w_in, *extra, kt, v, w_out, g, b)


def _proj_in_body(x_ref, w_ref, o_ref):
    o_ref[...] = _dot(x_ref[...].astype(BF16), w_ref[...])


def _proj_in(x2, w):
    t, d = x2.shape
    n = w.shape[1]
    return pl.pallas_call(
        _proj_in_body,
        grid=(t // ROW_TILE,),
        in_specs=[pl.BlockSpec((ROW_TILE, d), lambda i: (i, 0)),
                  pl.BlockSpec((d, n), lambda i: (0, 0))],
        out_specs=pl.BlockSpec((ROW_TILE, n), lambda i: (i, 0)),
        out_shape=jax.ShapeDtypeStruct((t, n), F32),
        compiler_params=_params("arbitrary"),
        name="proj_in",
    )(x2, w)


def _alibi_slope(index, total):
    return 2.0 ** (-8.0 * (index + 1) / total)


def _rows(start, size, stride):
    return pl.ds(start, size) if stride == 1 else pl.ds(start, size, stride=stride)


def _load_cols(refs, rows):
    return jnp.concatenate([r[rows, :] for r in refs], axis=1)


def _store_cols(refs, rows, val):
    for c, r in enumerate(refs):
        r[rows, :] = val[:, c * LANES:(c + 1) * LANES]


def _dilated_group(gi, dilation, seq_len, hq_refs, hk_refs, hv_refs, m_refs, l_refs, o_refs):
    n_sub = seq_len // dilation
    qb = min(Q_BLOCK, n_sub)
    kw = min(qb + 2 * DIL_RADIUS, n_sub)
    masks = _head_masks(DIL_WIDTH)
    n_heads_total = len(DIL_GROUPS) * DIL_HEADS
    bias_cache = {}

    def bias_and_mask(offset):
        if offset not in bias_cache:
            rel = offset + lax.broadcasted_iota(jnp.int32, (qb, kw), 0) - lax.broadcasted_iota(jnp.int32, (qb, kw), 1)
            dist = jnp.abs(rel)
            inside = dist <= DIL_RADIUS
            span = (dist * dilation).astype(F32)
            bias = jnp.concatenate(
                [-_alibi_slope(gi * DIL_HEADS + h, n_heads_total) * span for h in range(DIL_HEADS)], axis=0)
            bias_cache[offset] = (bias, jnp.concatenate([inside] * DIL_HEADS, axis=0))
        return bias_cache[offset]

    for r in range(dilation):
        for j0 in range(0, n_sub, qb):
            ks = min(max(j0 - DIL_RADIUS, 0), n_sub - kw)
            q_rows = _rows(r + j0 * dilation, qb, dilation)
            k_rows = _rows(r + ks * dilation, kw, dilation)
            q = _load_cols(hq_refs, q_rows)
            k = _load_cols(hk_refs, k_rows).astype(BF16)
            v = _load_cols(hv_refs, k_rows).astype(BF16)
            sc = lax.dot_general(_stack_heads(q, masks), k, (((1,), (1,)), ((), ())),
                                 preferred_element_type=F32) * HEAD_DIM ** -0.5
            bias, inside = bias_and_mask(j0 - ks)
            sc = jnp.where(inside, sc + bias, NEG_INF)
            mx = jnp.max(sc, axis=-1, keepdims=True)
            p = jnp.exp(sc - mx)
            den = jnp.sum(p, axis=-1, keepdims=True)
            num = _unstack_heads(_dot(p.astype(BF16), v), masks, qb)
            mx = _unstack_heads(jnp.broadcast_to(mx, (DIL_HEADS * qb, DIL_WIDTH)), masks, qb)
            den = _unstack_heads(jnp.broadcast_to(den, (DIL_HEADS * qb, DIL_WIDTH)), masks, qb)
            if gi > 0:
                m_old = _load_cols(m_refs, q_rows)
                m_new = jnp.maximum(m_old, mx)
                a_old = jnp.exp(m_old - m_new)
                a_new = jnp.exp(mx - m_new)
                mx = m_new
                den = _load_cols(l_refs, q_rows) * a_old + den * a_new
                num = _load_cols(o_refs, q_rows) * a_old + num * a_new
            _store_cols(m_refs, q_rows, mx)
            _store_cols(l_refs, q_rows, den)
            _store_cols(o_refs, q_rows, num)


def _attention_body(*refs):
    n_slab = DIL_WIDTH // LANES
    hq_refs, hk_refs, hv_refs = (refs[i * n_slab:(i + 1) * n_slab] for i in range(3))
    hm_ref, kt_ref, v_ref, out_ref = refs[3 * n_slab:3 * n_slab + 4]
    m_refs, l_refs, o_refs = (refs[3 * n_slab + 4 + i * n_slab:3 * n_slab + 4 + (i + 1) * n_slab] for i in range(3))
    g = pl.program_id(1)
    seq_len = hm_ref.shape[0]

    @pl.when(g == 0)
    def _():
        for c in range(0, seq_len, ROW_TILE):
            mem_out = _mem_attention(hm_ref[c:c + ROW_TILE, :], kt_ref[...], v_ref[...])
            out_ref[c:c + ROW_TILE, DIL_WIDTH:] = mem_out.astype(BF16)

    for gi, (_, dilation) in enumerate(DIL_GROUPS):
        @pl.when(g == gi)
        def _(gi=gi, dilation=dilation):
            _dilated_group(gi, dilation, seq_len, hq_refs, hk_refs, hv_refs, m_refs, l_refs, o_refs)

    @pl.when(g == len(DIL_GROUPS) - 1)
    def _():
        for c in range(n_slab):
            out_ref[:, c * LANES:(c + 1) * LANES] = (o_refs[c][...] / l_refs[c][...]).astype(BF16)


def _attention(h, kt, v):
    bsz, seq_len, _ = h.shape
    n_g = len(DIL_GROUPS)
    n_slab = DIL_WIDTH // LANES

    def slabs(base):
        return [pl.BlockSpec((None, seq_len, LANES), lambda i, g, c=c: (i, 0, (base + g) * n_slab + c))
                for c in range(n_slab)]

    return pl.pallas_call(
        _attention_body,
        grid=(bsz, n_g),
        in_specs=slabs(0) + slabs(n_g) + slabs(2 * n_g) + [
            pl.BlockSpec((None, seq_len, MEM_WIDTH), lambda i, g: (i, 0, 3 * n_g)),
            pl.BlockSpec((None,) + kt.shape[1:], lambda i, g: (i, 0, 0)),
            pl.BlockSpec((None,) + v.shape[1:], lambda i, g: (i, 0, 0))],
        out_specs=pl.BlockSpec((None, seq_len, DIL_WIDTH + MEM_WIDTH), lambda i, g: (i, 0, 0)),
        out_shape=jax.ShapeDtypeStruct((bsz, seq_len, DIL_WIDTH + MEM_WIDTH), BF16),
        scratch_shapes=[pltpu.VMEM((seq_len, LANES), F32)] * (3 * n_slab),
        compiler_params=_params("arbitrary", "arbitrary"),
        name="dilated_attention",
    )(*([h] * (3 * n_slab + 1)), kt, v)


def _proj_out_ln_body(a_ref, x_ref, w_ref, g_ref, b_ref, o_ref):
    y = _dot(a_ref[...], w_ref[...])
    o_ref[...] = _layer_norm(ALPHA * x_ref[...] + y, g_ref[...], b_ref[...])


def _proj_out_ln(a2, x2, w, g, b):
    t, d = x2.shape
    k = a2.shape[1]
    return pl.pallas_call(
        _proj_out_ln_body,
        grid=(t // ROW_TILE,),
        in_specs=[pl.BlockSpec((ROW_TILE, k), lambda i: (i, 0)),
                  pl.BlockSpec((ROW_TILE, d), lambda i: (i, 0)),
                  pl.BlockSpec((k, d), lambda i: (0, 0)),
                  pl.BlockSpec((1, d), lambda i: (0, 0)),
                  pl.BlockSpec((1, d), lambda i: (0, 0))],
        out_specs=pl.BlockSpec((ROW_TILE, d), lambda i: (i, 0)),
        out_shape=jax.ShapeDtypeStruct((t, d), F32),
        compiler_params=_params("arbitrary"),
        name="proj_out_ln",
    )(a2, x2, w, g, b)


def _swiglu(xb, wg, wu, wd):
    g = _dot(xb, wg)
    u = _dot(xb, wu)
    return _dot((g * jax.nn.sigmoid(g) * u).astype(BF16), wd)


def _ffn_ln_body(f_chunk, x_ref, wg_ref, wu_ref, wd_ref, g_ref, b_ref, o_ref):
    x = x_ref[...]
    xb = x.astype(BF16)
    y = None
    for c in range(0, wg_ref.shape[1], f_chunk):
        part = _swiglu(xb, wg_ref[:, c:c + f_chunk], wu_ref[:, c:c + f_chunk], wd_ref[c:c + f_chunk, :])
        y = part if y is None else y + part
    o_ref[...] = _layer_norm(ALPHA * x + y, g_ref[...], b_ref[...])


def _ffn_ln(x2, wg, wu, wd, g, b):
    t, d = x2.shape
    f = wg.shape[1]
    resident = lambda shape: pl.BlockSpec(shape, lambda i: (0, 0), pipeline_mode=pl.Buffered(1))
    return pl.pallas_call(
        functools.partial(_ffn_ln_body, f // 2),
        grid=(t // ROW_TILE,),
        in_specs=[pl.BlockSpec((ROW_TILE, d), lambda i: (i, 0)),
                  resident((d, f)), resident((d, f)), resident((f, d)),
                  pl.BlockSpec((1, d), lambda i: (0, 0)),
                  pl.BlockSpec((1, d), lambda i: (0, 0))],
        out_specs=pl.BlockSpec((ROW_TILE, d), lambda i: (i, 0)),
        out_shape=jax.ShapeDtypeStruct((t, d), F32),
        compiler_params=_params("arbitrary"),
        name="ffn_ln",
    )(x2, wg, wu, wd, g, b)


_COL_E0, _COL_E1, _COL_R0, _COL_R1, _COL_W0, _COL_W1 = range(6)


def _router_body(x_ref, wr_ref, tri_ref, meta_ref, cnt_ref, carry_ref):
    i = pl.program_id(0)

    @pl.when(i == 0)
    def _():
        carry_ref[...] = jnp.zeros_like(carry_ref)

    tm = x_ref.shape[0]
    lane = lax.broadcasted_iota(jnp.int32, (tm, LANES), 1).astype(F32)
    logits = jnp.dot(x_ref[...], wr_ref[...], preferred_element_type=F32, precision=lax.Precision.HIGHEST)
    logits = jnp.where(lane < N_EXPERTS, logits, -jnp.inf)
    m0 = jnp.max(logits, axis=-1, keepdims=True)
    e0 = jnp.min(jnp.where(logits == m0, lane, float(LANES)), axis=-1, keepdims=True)
    rest = jnp.where(lane == e0, -jnp.inf, logits)
    m1 = jnp.max(rest, axis=-1, keepdims=True)
    e1 = jnp.min(jnp.where(rest == m1, lane, float(LANES)), axis=-1, keepdims=True)
    ex = jnp.exp(m1 - m0)
    w0 = 1.0 / (1.0 + ex)
    w1 = ex / (1.0 + ex)
    hit0 = lane == e0
    hit1 = lane == e1
    onehot = (hit0 | hit1).astype(F32)
    before = _dot(tri_ref[...], onehot.astype(BF16)) + carry_ref[...]
    r0 = jnp.sum(jnp.where(hit0, before, 0.0), axis=-1, keepdims=True)
    r1 = jnp.sum(jnp.where(hit1, before, 0.0), axis=-1, keepdims=True)
    carry_ref[...] += jnp.sum(onehot, axis=0, keepdims=True)
    cnt_ref[...] = carry_ref[...]
    meta = jnp.zeros((tm, LANES), F32)
    for col, val in ((_COL_E0, e0), (_COL_E1, e1), (_COL_R0, r0), (_COL_R1, r1), (_COL_W0, w0), (_COL_W1, w1)):
        meta = jnp.where(lane == col, val, meta)
    meta_ref[...] = meta


def _router(x2, w_router):
    t, d = x2.shape
    wr = jnp.zeros((d, LANES), F32).at[:, :N_EXPERTS].set(w_router)
    tri = jnp.tril(jnp.ones((ROW_TILE, ROW_TILE), BF16), -1)
    return pl.pallas_call(
        _router_body,
        grid=(t // ROW_TILE,),
        in_specs=[pl.BlockSpec((ROW_TILE, d), lambda i: (i, 0)),
                  pl.BlockSpec((d, LANES), lambda i: (0, 0)),
                  pl.BlockSpec((ROW_TILE, ROW_TILE), lambda i: (0, 0))],
        out_specs=[pl.BlockSpec((ROW_TILE, LANES), lambda i: (i, 0)),
                   pl.BlockSpec((1, LANES), lambda i: (0, 0))],
        out_shape=[jax.ShapeDtypeStruct((t, LANES), F32), jax.ShapeDtypeStruct((1, LANES), F32)],
        scratch_shapes=[pltpu.VMEM((1, LANES), F32)],
        compiler_params=_params("arbitrary"),
        name="router",
    )(x2, wr, tri)


def _row_copy(src_ref, src_row, dst_ref, dst_row, sem):
    return pltpu.make_async_copy(src_ref.at[pl.ds(src_row, 1), :], dst_ref.at[pl.ds(dst_row, 1), :], sem)


def _dispatch_body(dest_ref, x_ref, xg_in_ref, xg_ref, sem):
    del xg_in_ref
    tm = x_ref.shape[0]

    def issue(i, _):
        for k in range(TOP_K):
            _row_copy(x_ref, i, xg_ref, dest_ref[0, 0, TOP_K * i + k], sem).start()
        return 0

    lax.fori_loop(0, tm, issue, 0)

    def drain(i, _):
        for k in range(TOP_K):
            _row_copy(x_ref, i, xg_ref, dest_ref[0, 0, TOP_K * i + k], sem).wait()
        return 0

    lax.fori_loop(0, tm, drain, 0)


def _dispatch(x2, dest, n_rows):
    t, d = x2.shape
    n_tiles = t // ROW_TILE
    return pl.pallas_call(
        _dispatch_body,
        grid=(n_tiles,),
        in_specs=[pl.BlockSpec((1, 1, TOP_K * ROW_TILE), lambda i: (i, 0, 0), memory_space=pltpu.SMEM),
                  pl.BlockSpec((ROW_TILE, d), lambda i: (i, 0)),
                  pl.BlockSpec(memory_space=pl.ANY)],
        out_specs=pl.BlockSpec(memory_space=pl.ANY),
        out_shape=jax.ShapeDtypeStruct((n_rows, d), F32),
        scratch_shapes=[pltpu.SemaphoreType.DMA(())],
        input_output_aliases={2: 0},
        compiler_params=_params("arbitrary"),
        name="dispatch",
    )(dest.reshape(n_tiles, 1, TOP_K * ROW_TILE), x2, jnp.zeros((n_rows, d), F32))


def _experts_body(be_ref, nused_ref, x_ref, wg_ref, wu_ref, wd_ref, o_ref):
    blk = pl.program_id(0)
    j = pl.program_id(1)

    @pl.when(blk < nused_ref[0])
    def _():
        y = _swiglu(x_ref[...].astype(BF16), wg_ref[...], wu_ref[...], wd_ref[...])

        @pl.when(j == 0)
        def _():
            o_ref[...] = y

        @pl.when(j > 0)
        def _():
            o_ref[...] += y

    @pl.when(blk >= nused_ref[0])
    def _():
        o_ref[...] = jnp.zeros_like(o_ref)


def _experts(xg, block_e, n_used, wg, wu, wd):
    n_rows, d = xg.shape
    f = wg.shape[2]
    ft = f // EXPERT_F_TILES
    n_blocks = n_rows // EXPERT_BLOCK
    last = EXPERT_F_TILES - 1

    def used(blk, nu):
        return jnp.minimum(blk, nu[0] - 1)

    def f_tile(blk, j, nu):
        return jnp.where(blk < nu[0], j, last)

    grid_spec = pltpu.PrefetchScalarGridSpec(
        num_scalar_prefetch=2,
        grid=(n_blocks, EXPERT_F_TILES),
        in_specs=[pl.BlockSpec((EXPERT_BLOCK, d), lambda blk, j, be, nu: (used(blk, nu), 0)),
                  pl.BlockSpec((None, d, ft), lambda blk, j, be, nu: (be[used(blk, nu)], 0, f_tile(blk, j, nu))),
                  pl.BlockSpec((None, d, ft), lambda blk, j, be, nu: (be[used(blk, nu)], 0, f_tile(blk, j, nu))),
                  pl.BlockSpec((None, ft, d), lambda blk, j, be, nu: (be[used(blk, nu)], f_tile(blk, j, nu), 0))],
        out_specs=pl.BlockSpec((EXPERT_BLOCK, d), lambda blk, j, be, nu: (blk, 0)),
    )
    return pl.pallas_call(
        _experts_body,
        grid_spec=grid_spec,
        out_shape=jax.ShapeDtypeStruct((n_rows, d), F32),
        compiler_params=_params("arbitrary", "arbitrary"),
        name="experts",
    )(block_e, n_used, xg, wg, wu, wd)


def _combine_ln_body(dest_ref, x_ref, w_ref, yb_ref, g_ref, b_ref, o_ref, buf_ref, sem):
    tm = x_ref.shape[0]

    def issue(i, _):
        for k in range(TOP_K):
            _row_copy(yb_ref, dest_ref[0, 0, TOP_K * i + k], buf_ref.at[k], i, sem).start()
        return 0

    lax.fori_loop(0, tm, issue, 0)

    def drain(i, _):
        for k in range(TOP_K):
            _row_copy(yb_ref, dest_ref[0, 0, TOP_K * i + k], buf_ref.at[k], i, sem).wait()
        return 0

    lax.fori_loop(0, tm, drain, 0)
    w = w_ref[...]
    y = w[:, _COL_W0:_COL_W0 + 1] * buf_ref[0] + w[:, _COL_W1:_COL_W1 + 1] * buf_ref[1]
    o_ref[...] = _layer_norm(ALPHA * x_ref[...] + y, g_ref[...], b_ref[...])


def _combine_ln(x2, meta, yb, dest, g, b):
    t, d = x2.shape
    n_tiles = t // ROW_TILE
    return pl.pallas_call(
        _combine_ln_body,
        grid=(n_tiles,),
        in_specs=[pl.BlockSpec((1, 1, TOP_K * ROW_TILE), lambda i: (i, 0, 0), memory_space=pltpu.SMEM),
                  pl.BlockSpec((ROW_TILE, d), lambda i: (i, 0)),
                  pl.BlockSpec((ROW_TILE, LANES), lambda i: (i, 0)),
                  pl.BlockSpec(memory_space=pl.ANY),
                  pl.BlockSpec((1, d), lambda i: (0, 0)),
                  pl.BlockSpec((1, d), lambda i: (0, 0))],
        out_specs=pl.BlockSpec((ROW_TILE, d), lambda i: (i, 0)),
        out_shape=jax.ShapeDtypeStruct((t, d), F32),
        scratch_shapes=[pltpu.VMEM((TOP_K, ROW_TILE, d), F32), pltpu.SemaphoreType.DMA(())],
        compiler_params=_params("arbitrary"),
        name="combine_ln",
    )(dest.reshape(n_tiles, 1, TOP_K * ROW_TILE), x2, meta, yb, g, b)


def _moe_ln(x2, w_router, wg, wu, wd, g, b):
    t, _ = x2.shape
    meta, counts = _router(x2, w_router)
    counts = counts[0, :N_EXPERTS].astype(jnp.int32)
    padded = (counts + EXPERT_BLOCK - 1) // EXPERT_BLOCK * EXPERT_BLOCK
    pad_end = jnp.cumsum(padded)
    pad_start = pad_end - padded
    experts = meta[:, _COL_E0:_COL_E1 + 1].astype(jnp.int32)
    ranks = meta[:, _COL_R0:_COL_R1 + 1].astype(jnp.int32)
    dest = pad_start[experts] + ranks
    n_blocks = t * TOP_K // EXPERT_BLOCK + N_EXPERTS
    block_start = jnp.arange(n_blocks, dtype=jnp.int32) * EXPERT_BLOCK
    block_e = jnp.minimum(jnp.sum(pad_end[None, :] <= block_start[:, None], axis=1), N_EXPERTS - 1).astype(jnp.int32)
    n_used = (pad_end[-1:] // EXPERT_BLOCK).astype(jnp.int32)
    xg = _dispatch(x2, dest, n_blocks * EXPERT_BLOCK)
    yb = _experts(xg, block_e, n_used, wg, wu, wd)
    return _combine_ln(x2, meta, yb, dest, g, b)


def _block_diag(w):
    n_g, c, _ = w.shape
    out = jnp.zeros((n_g * c, n_g * c), w.dtype)
    for gi in range(n_g):
        out = out.at[gi * c:(gi + 1) * c, gi * c:(gi + 1) * c].set(w[gi])
    return out


def kernel(x, mem, w_mem_kv, a_w_in, a_conv_w, a_w_out, b_w_in, b_w_out, c_w_in, c_pool_w, c_pool_scale, c_w_out, ln_g, ln_b, ffn_w_gate, ffn_w_up, ffn_w_down, moe_router, moe_w_gate, moe_w_up, moe_w_down):
    bsz, seq_len, d = x.shape
    t = bsz * seq_len
    kt, v = _mem_kv(mem, w_mem_kv.astype(BF16))
    x = x.astype(F32)
    for i in range(DEPTH):
        kind, j = i % 3, i // 3
        g0, b0 = ln_g[i, 0].reshape(1, d), ln_b[i, 0].reshape(1, d)
        g1, b1 = ln_g[i, 1].reshape(1, d), ln_b[i, 1].reshape(1, d)
        if kind == 0:
            x = _mixer_layer(_conv_layer_body, x, a_w_in[j].astype(BF16), [a_conv_w[j]], kt, v,
                             a_w_out[j].astype(BF16), g0, b0)
        elif kind == 1:
            h = _proj_in(x.reshape(t, d), b_w_in[j].astype(BF16)).reshape(bsz, seq_len, -1)
            a = _attention(h, kt, v)
            x = _proj_out_ln(a.reshape(t, -1), x.reshape(t, d), b_w_out[j].astype(BF16), g0, b0)
            x = x.reshape(bsz, seq_len, d)
        else:
            x = _mixer_layer(_pool_layer_body, x, c_w_in[j].astype(BF16),
                             [_block_diag(c_pool_w[j]).astype(BF16), c_pool_scale[j].reshape(1, -1)], kt, v,
                             c_w_out[j].astype(BF16), g0, b0)
        f = i // 2
        x2 = x.reshape(t, d)
        if i % 2 == 0:
            x2 = _ffn_ln(x2, ffn_w_gate[f].astype(BF16), ffn_w_up[f].astype(BF16), ffn_w_down[f].astype(BF16),
                         g1, b1)
        else:
            x2 = _moe_ln(x2, moe_router[f], moe_w_gate[f].astype(BF16), moe_w_up[f].astype(BF16),
                         moe_w_down[f].astype(BF16), g1, b1)
        x = x2.reshape(bsz, seq_len, d)
    return x
```

```python
import functools

import jax
import jax.numpy as jnp
from jax import lax
from jax.experimental import pallas as pl
from jax.experimental.pallas import tpu as pltpu

F32 = jnp.float32
BF16 = jnp.bfloat16

D_MODEL = 1024
DEPTH = 4
HEAD_DIM = 64
MEM_HEADS = 4
MEM_WIDTH = MEM_HEADS * HEAD_DIM
MIX_WIDTH = 3 * D_MODEL // 4
DIL_GROUPS = ((128, 1), (512, 4), (2048, 16))
DIL_HEADS = 4
DIL_WIDTH = DIL_HEADS * HEAD_DIM
DIL_RADIUS = 64
POOL_WINDOWS = (2, 4, 8, 16)
POOL_GROUP = MIX_WIDTH // len(POOL_WINDOWS)
N_EXPERTS = 8
TOP_K = 2
LN_EPS = 1e-5
NEG_INF = -1e30
ALPHA = (2 * DEPTH) ** 0.25

LANES = 128
HALO = 8
SEQ_TILE = 512
ROW_TILE = 512
EXPERT_BLOCK = 512
EXPERT_F_TILES = 2
Q_BLOCK = 128
ISSUE_UNROLL = 8
VMEM_LIMIT = 56 * 1024 * 1024


def _params(*sem):
    return pltpu.CompilerParams(dimension_semantics=sem, vmem_limit_bytes=VMEM_LIMIT)


def _layer_norm(v, g, b):
    mu = jnp.mean(v, axis=-1, keepdims=True)
    d = v - mu
    var = jnp.mean(d * d, axis=-1, keepdims=True)
    return d * lax.rsqrt(var + LN_EPS) * g + b


def _dot(a, b):
    return jnp.dot(a, b, preferred_element_type=F32)


def _head_masks(width):
    col = lax.broadcasted_iota(jnp.int32, (1, width), 1)
    return [(col >= h * HEAD_DIM) & (col < (h + 1) * HEAD_DIM) for h in range(width // HEAD_DIM)]


def _stack_heads(q, masks):
    return jnp.concatenate([jnp.where(m, q, 0.0) for m in masks], axis=0).astype(BF16)


def _unstack_heads(o, masks, n):
    out = o[(len(masks) - 1) * n:]
    for h in range(len(masks) - 2, -1, -1):
        out = jnp.where(masks[h], o[h * n:(h + 1) * n], out)
    return out


def _mem_attention(q, kt, v):
    n = q.shape[0]
    masks = _head_masks(MEM_WIDTH)
    sc = _dot(_stack_heads(q, masks), kt) * HEAD_DIM ** -0.5
    p = jnp.exp(sc - jnp.max(sc, axis=-1, keepdims=True))
    p = p / jnp.sum(p, axis=-1, keepdims=True)
    return _unstack_heads(_dot(p.astype(BF16), v), masks, n)


def _mem_kv_body(mem_ref, w_ref, kt_ref, v_ref):
    kv = _dot(mem_ref[...].astype(BF16), w_ref[...])
    kt_ref[...] = kv[:, :MEM_WIDTH].T.astype(BF16)
    v_ref[...] = kv[:, MEM_WIDTH:].astype(BF16)


def _mem_kv(mem, w_kv):
    b, m, d = mem.shape
    return pl.pallas_call(
        _mem_kv_body,
        grid=(b,),
        in_specs=[pl.BlockSpec((None, m, d), lambda i: (i, 0, 0)),
                  pl.BlockSpec((d, 2 * MEM_WIDTH), lambda i: (0, 0))],
        out_specs=[pl.BlockSpec((None, MEM_WIDTH, m), lambda i: (i, 0, 0)),
                   pl.BlockSpec((None, m, MEM_WIDTH), lambda i: (i, 0, 0))],
        out_shape=[jax.ShapeDtypeStruct((b, MEM_WIDTH, m), BF16),
                   jax.ShapeDtypeStruct((b, m, MEM_WIDTH), BF16)],
        compiler_params=_params("arbitrary"),
        name="mem_kv",
    )(mem, w_kv)


def _ext_rows(xp_ref, xm_ref, xn_ref, seq_len):
    s = pl.program_id(1)
    ts = xm_ref.shape[0]
    xm = xm_ref[...]
    xe = jnp.concatenate([xp_ref[...], xm, xn_ref[...]], axis=0).astype(BF16)
    pos = s * ts - HALO + lax.broadcasted_iota(jnp.int32, (ts + 2 * HALO, 1), 0)
    valid = (pos >= 0) & (pos < seq_len)
    return xm, xe, valid


def _shift_rows(a, k):
    n = a.shape[0]
    return pltpu.roll(a, k % n, 0)


def _mixer_tail(mix, q_mem, xm, kt_ref, v_ref, wout_ref, g_ref, b_ref, o_ref):
    mem_out = _mem_attention(q_mem, kt_ref[...], v_ref[...])
    y = (_dot(mix.astype(BF16), wout_ref[:MIX_WIDTH, :])
         + _dot(mem_out.astype(BF16), wout_ref[MIX_WIDTH:, :]))
    o_ref[...] = _layer_norm(ALPHA * xm + y, g_ref[...], b_ref[...])


def _conv_layer_body(seq_len, xp_ref, xm_ref, xn_ref, win_ref, cw_ref, kt_ref, v_ref,
                     wout_ref, g_ref, b_ref, o_ref):
    ts = xm_ref.shape[0]
    xm, xe, valid = _ext_rows(xp_ref, xm_ref, xn_ref, seq_len)
    xmb = xm.astype(BF16)
    gate_b = _dot(xmb, win_ref[:, :MIX_WIDTH])
    cu = _dot(xe, win_ref[:, MIX_WIDTH:3 * MIX_WIDTH])
    q_mem = _dot(xmb, win_ref[:, 3 * MIX_WIDTH:])
    z = jnp.where(valid, cu[:, :MIX_WIDTH] * cu[:, MIX_WIDTH:], 0.0)
    cw = cw_ref[...]
    conv = (cw[0:1] * _shift_rows(z, 1)[HALO:HALO + ts]
            + cw[1:2] * z[HALO:HALO + ts]
            + cw[2:3] * _shift_rows(z, -1)[HALO:HALO + ts])
    _mixer_tail(gate_b * conv, q_mem, xm, kt_ref, v_ref, wout_ref, g_ref, b_ref, o_ref)


def _pool_layer_body(seq_len, xp_ref, xm_ref, xn_ref, win_ref, pw_ref, ps_ref, kt_ref, v_ref,
                     wout_ref, g_ref, b_ref, o_ref):
    ts = xm_ref.shape[0]
    s = pl.program_id(1)
    xm, xe, valid = _ext_rows(xp_ref, xm_ref, xn_ref, seq_len)
    u = jnp.where(valid, _dot(xe, win_ref[:, :MIX_WIDTH]), 0.0)
    q_mem = _dot(xm.astype(BF16), win_ref[:, MIX_WIDTH:])
    a2 = u + _shift_rows(u, 1)
    a4 = _shift_rows(a2, 1) + _shift_rows(a2, -1)
    a8 = _shift_rows(a4, 2) + _shift_rows(a4, -2)
    a16 = _shift_rows(a8, 4) + _shift_rows(a8, -4)
    col = lax.broadcasted_iota(jnp.int32, (1, MIX_WIDTH), 1)
    pos = s * ts + lax.broadcasted_iota(jnp.int32, (ts, 1), 0)
    num = a16[HALO:HALO + ts]
    cnt = None
    for gi in range(len(POOL_WINDOWS) - 1, -1, -1):
        w = POOL_WINDOWS[gi]
        c_w = (jnp.minimum(pos + (w // 2 - 1), seq_len - 1) - jnp.maximum(pos - w // 2, 0) + 1).astype(F32)
        if cnt is None:
            cnt = jnp.broadcast_to(c_w, (ts, MIX_WIDTH))
        else:
            in_group = col < (gi + 1) * POOL_GROUP
            num = jnp.where(in_group, (a2, a4, a8)[gi][HALO:HALO + ts], num)
            cnt = jnp.where(in_group, c_w, cnt)
    diff = num / cnt - u[HALO:HALO + ts]
    mix = _dot(diff.astype(BF16), pw_ref[...]) * ps_ref[...]
    _mixer_tail(mix, q_mem, xm, kt_ref, v_ref, wout_ref, g_ref, b_ref, o_ref)


def _whole(w, layer=None):
    if layer is None:
        return pl.BlockSpec(w.shape, lambda *_: (0,) * w.ndim)
    return pl.BlockSpec((None,) + w.shape[1:], lambda *_: (layer,) + (0,) * (w.ndim - 1))


def _mixer_layer(body, layer, x, w_in, extra, kt, v, w_out, g, b):
    bsz, seq_len, d = x.shape
    ts = SEQ_TILE
    n_halo_blocks = seq_len // HALO
    in_specs = [
        pl.BlockSpec((None, HALO, d), lambda i, s: (i, jnp.maximum(s * (ts // HALO) - 1, 0), 0)),
        pl.BlockSpec((None, ts, d), lambda i, s: (i, s, 0)),
        pl.BlockSpec((None, HALO, d), lambda i, s: (i, jnp.minimum((s + 1) * (ts // HALO), n_halo_blocks - 1), 0)),
        _whole(w_in, layer),
    ]
    in_specs += [_whole(e) for e in extra]
    in_specs += [
        pl.BlockSpec((None,) + kt.shape[1:], lambda i, s: (i, 0, 0)),
        pl.BlockSpec((None,) + v.shape[1:], lambda i, s: (i, 0, 0)),
        _whole(w_out, layer),
        _whole(g),
        _whole(b),
    ]
    return pl.pallas_call(
        functools.partial(body, seq_len),
        grid=(bsz, seq_len // ts),
        in_specs=in_specs,
        out_specs=pl.BlockSpec((None, ts, d), lambda i, s: (i, s, 0)),
        out_shape=jax.ShapeDtypeStruct(x.shape, F32),
        compiler_params=_params("arbitrary", "arbitrary"),
        name=body.__name__.strip("_"),
    )(x, x, x, w_in, *extra, kt, v, w_out, g, b)


def _proj_in_body(x_ref, w_ref, o_ref):
    o_ref[...] = _dot(x_ref[...].astype(BF16), w_ref[...])


def _proj_in(layer, x2, w):
    t, d = x2.shape
    n = w.shape[-1]
    return pl.pallas_call(
        _proj_in_body,
        grid=(t // ROW_TILE,),
        in_specs=[pl.BlockSpec((ROW_TILE, d), lambda i: (i, 0)),
                  _whole(w, layer)],
        out_specs=pl.BlockSpec((ROW_TILE, n), lambda i: (i, 0)),
        out_shape=jax.ShapeDtypeStruct((t, n), F32),
        compiler_params=_params("arbitrary"),
        name="proj_in",
    )(x2, w)


def _alibi_slope(index, total):
    return 2.0 ** (-8.0 * (index + 1) / total)


def _rows(start, size, stride):
    return pl.ds(start, size) if stride == 1 else pl.ds(start, size, stride=stride)


def _load_cols(refs, rows):
    return jnp.concatenate([r[rows, :] for r in refs], axis=1)


def _store_cols(refs, rows, val):
    for c, r in enumerate(refs):
        r[rows, :] = val[:, c * LANES:(c + 1) * LANES]


def _dilated_group(gi, dilation, seq_len, hq_refs, hk_refs, hv_refs, m_refs, l_refs, o_refs):
    n_sub = seq_len // dilation
    qb = min(Q_BLOCK, n_sub)
    kw = min(qb + 2 * DIL_RADIUS, n_sub)
    masks = _head_masks(DIL_WIDTH)
    n_heads_total = len(DIL_GROUPS) * DIL_HEADS
    bias_cache = {}

    def bias_and_mask(offset):
        if offset not in bias_cache:
            rel = offset + lax.broadcasted_iota(jnp.int32, (qb, kw), 0) - lax.broadcasted_iota(jnp.int32, (qb, kw), 1)
            dist = jnp.abs(rel)
            inside = dist <= DIL_RADIUS
            span = (dist * dilation).astype(F32)
            bias = jnp.concatenate(
                [-_alibi_slope(gi * DIL_HEADS + h, n_heads_total) * span for h in range(DIL_HEADS)], axis=0)
            bias_cache[offset] = (bias, jnp.concatenate([inside] * DIL_HEADS, axis=0))
        return bias_cache[offset]

    for r in range(dilation):
        for j0 in range(0, n_sub, qb):
            ks = min(max(j0 - DIL_RADIUS, 0), n_sub - kw)
            q_rows = _rows(r + j0 * dilation, qb, dilation)
            k_rows = _rows(r + ks * dilation, kw, dilation)
            q = _load_cols(hq_refs, q_rows)
            k = _load_cols(hk_refs, k_rows).astype(BF16)
            v = _load_cols(hv_refs, k_rows).astype(BF16)
            sc = lax.dot_general(_stack_heads(q, masks), k, (((1,), (1,)), ((), ())),
                                 preferred_element_type=F32) * HEAD_DIM ** -0.5
            bias, inside = bias_and_mask(j0 - ks)
            sc = jnp.where(inside, sc + bias, NEG_INF)
            mx = jnp.max(sc, axis=-1, keepdims=True)
            p = jnp.exp(sc - mx)
            den = jnp.sum(p, axis=-1, keepdims=True)
            num = _unstack_heads(_dot(p.astype(BF16), v), masks, qb)
            mx = _unstack_heads(jnp.broadcast_to(mx, (DIL_HEADS * qb, DIL_WIDTH)), masks, qb)
            den = _unstack_heads(jnp.broadcast_to(den, (DIL_HEADS * qb, DIL_WIDTH)), masks, qb)
            if gi > 0:
                m_old = _load_cols(m_refs, q_rows)
                m_new = jnp.maximum(m_old, mx)
                a_old = jnp.exp(m_old - m_new)
                a_new = jnp.exp(mx - m_new)
                mx = m_new
                den = _load_cols(l_refs, q_rows) * a_old + den * a_new
                num = _load_cols(o_refs, q_rows) * a_old + num * a_new
            _store_cols(m_refs, q_rows, mx)
            _store_cols(l_refs, q_rows, den)
            _store_cols(o_refs, q_rows, num)


def _attention_body(*refs):
    n_slab = DIL_WIDTH // LANES
    hq_refs, hk_refs, hv_refs = (refs[i * n_slab:(i + 1) * n_slab] for i in range(3))
    hm_ref, kt_ref, v_ref, out_ref = refs[3 * n_slab:3 * n_slab + 4]
    m_refs, l_refs, o_refs = (refs[3 * n_slab + 4 + i * n_slab:3 * n_slab + 4 + (i + 1) * n_slab] for i in range(3))
    g = pl.program_id(1)
    seq_len = hm_ref.shape[0]

    @pl.when(g == 0)
    def _():
        for c in range(0, seq_len, ROW_TILE):
            mem_out = _mem_attention(hm_ref[c:c + ROW_TILE, :], kt_ref[...], v_ref[...])
            out_ref[c:c + ROW_TILE, DIL_WIDTH:] = mem_out.astype(BF16)

    for gi, (_, dilation) in enumerate(DIL_GROUPS):
        @pl.when(g == gi)
        def _(gi=gi, dilation=dilation):
            _dilated_group(gi, dilation, seq_len, hq_refs, hk_refs, hv_refs, m_refs, l_refs, o_refs)

    @pl.when(g == len(DIL_GROUPS) - 1)
    def _():
        for c in range(n_slab):
            out_ref[:, c * LANES:(c + 1) * LANES] = (o_refs[c][...] / l_refs[c][...]).astype(BF16)


def _attention(h, kt, v):
    bsz, seq_len, _ = h.shape
    n_g = len(DIL_GROUPS)
    n_slab = DIL_WIDTH // LANES

    def slabs(base):
        return [pl.BlockSpec((None, seq_len, LANES), lambda i, g, c=c: (i, 0, (base + g) * n_slab + c))
                for c in range(n_slab)]

    return pl.pallas_call(
        _attention_body,
        grid=(bsz, n_g),
        in_specs=slabs(0) + slabs(n_g) + slabs(2 * n_g) + [
            pl.BlockSpec((None, seq_len, MEM_WIDTH), lambda i, g: (i, 0, 3 * n_g)),
            pl.BlockSpec((None,) + kt.shape[1:], lambda i, g: (i, 0, 0)),
            pl.BlockSpec((None,) + v.shape[1:], lambda i, g: (i, 0, 0))],
        out_specs=pl.BlockSpec((None, seq_len, DIL_WIDTH + MEM_WIDTH), lambda i, g: (i, 0, 0)),
        out_shape=jax.ShapeDtypeStruct((bsz, seq_len, DIL_WIDTH + MEM_WIDTH), BF16),
        scratch_shapes=[pltpu.VMEM((seq_len, LANES), F32)] * (3 * n_slab),
        compiler_params=_params("arbitrary", "arbitrary"),
        name="dilated_attention",
    )(*([h] * (3 * n_slab + 1)), kt, v)


def _proj_out_ln_body(a_ref, x_ref, w_ref, g_ref, b_ref, o_ref):
    y = _dot(a_ref[...], w_ref[...])
    o_ref[...] = _layer_norm(ALPHA * x_ref[...] + y, g_ref[...], b_ref[...])


def _proj_out_ln(layer, a2, x2, w, g, b):
    t, d = x2.shape
    k = a2.shape[1]
    return pl.pallas_call(
        _proj_out_ln_body,
        grid=(t // ROW_TILE,),
        in_specs=[pl.BlockSpec((ROW_TILE, k), lambda i: (i, 0)),
                  pl.BlockSpec((ROW_TILE, d), lambda i: (i, 0)),
                  _whole(w, layer), _whole(g), _whole(b)],
        out_specs=pl.BlockSpec((ROW_TILE, d), lambda i: (i, 0)),
        out_shape=jax.ShapeDtypeStruct((t, d), F32),
        compiler_params=_params("arbitrary"),
        name="proj_out_ln",
    )(a2, x2, w, g, b)


def _swiglu(xb, wg, wu, wd):
    g = _dot(xb, wg)
    u = _dot(xb, wu)
    return _dot((g * jax.nn.sigmoid(g) * u).astype(BF16), wd)


def _ffn_ln_body(f_chunk, x_ref, wg_ref, wu_ref, wd_ref, g_ref, b_ref, o_ref):
    x = x_ref[...]
    xb = x.astype(BF16)
    y = None
    for c in range(0, wg_ref.shape[1], f_chunk):
        part = _swiglu(xb, wg_ref[:, c:c + f_chunk], wu_ref[:, c:c + f_chunk], wd_ref[c:c + f_chunk, :])
        y = part if y is None else y + part
    o_ref[...] = _layer_norm(ALPHA * x + y, g_ref[...], b_ref[...])


def _ffn_ln(layer, x2, wg, wu, wd, g, b):
    t, d = x2.shape
    f = wg.shape[-1]
    resident = lambda w: pl.BlockSpec((None,) + w.shape[1:], lambda i: (layer, 0, 0), pipeline_mode=pl.Buffered(1))
    return pl.pallas_call(
        functools.partial(_ffn_ln_body, f // 2),
        grid=(t // ROW_TILE,),
        in_specs=[pl.BlockSpec((ROW_TILE, d), lambda i: (i, 0)),
                  resident(wg), resident(wu), resident(wd), _whole(g), _whole(b)],
        out_specs=pl.BlockSpec((ROW_TILE, d), lambda i: (i, 0)),
        out_shape=jax.ShapeDtypeStruct((t, d), F32),
        compiler_params=_params("arbitrary"),
        name="ffn_ln",
    )(x2, wg, wu, wd, g, b)


_COL_E0, _COL_E1, _COL_R0, _COL_R1, _COL_W0, _COL_W1 = range(6)


def _split_bf16(a):
    hi = a.astype(BF16)
    return hi, (a - hi.astype(F32)).astype(BF16)


def _router_body(x_ref, wh_ref, wl_ref, tri_ref, meta_ref, cnt_ref, carry_ref):
    i = pl.program_id(0)

    @pl.when(i == 0)
    def _():
        carry_ref[...] = jnp.zeros_like(carry_ref)

    tm = x_ref.shape[0]
    lane = lax.broadcasted_iota(jnp.int32, (tm, LANES), 1).astype(F32)
    xh, xl = _split_bf16(x_ref[...])
    logits = _dot(xh, wh_ref[...]) + (_dot(xl, wh_ref[...]) + _dot(xh, wl_ref[...]))
    logits = jnp.where(lane < N_EXPERTS, logits, -jnp.inf)
    m0 = jnp.max(logits, axis=-1, keepdims=True)
    e0 = jnp.min(jnp.where(logits == m0, lane, float(LANES)), axis=-1, keepdims=True)
    rest = jnp.where(lane == e0, -jnp.inf, logits)
    m1 = jnp.max(rest, axis=-1, keepdims=True)
    e1 = jnp.min(jnp.where(rest == m1, lane, float(LANES)), axis=-1, keepdims=True)
    ex = jnp.exp(m1 - m0)
    w0 = 1.0 / (1.0 + ex)
    w1 = ex / (1.0 + ex)
    hit0 = lane == e0
    hit1 = lane == e1
    onehot = (hit0 | hit1).astype(F32)
    before = _dot(tri_ref[...], onehot.astype(BF16)) + carry_ref[...]
    r0 = jnp.sum(jnp.where(hit0, before, 0.0), axis=-1, keepdims=True)
    r1 = jnp.sum(jnp.where(hit1, before, 0.0), axis=-1, keepdims=True)
    carry_ref[...] += jnp.sum(onehot, axis=0, keepdims=True)
    cnt_ref[...] = carry_ref[...]
    meta = jnp.zeros((tm, LANES), F32)
    for col, val in ((_COL_E0, e0), (_COL_E1, e1), (_COL_R0, r0), (_COL_R1, r1), (_COL_W0, w0), (_COL_W1, w1)):
        meta = jnp.where(lane == col, val, meta)
    meta_ref[...] = meta


def _router(x2, w_router):
    t, d = x2.shape
    wh, wl = _split_bf16(jnp.zeros((d, LANES), F32).at[:, :N_EXPERTS].set(w_router))
    tri = jnp.tril(jnp.ones((ROW_TILE, ROW_TILE), BF16), -1)
    return pl.pallas_call(
        _router_body,
        grid=(t // ROW_TILE,),
        in_specs=[pl.BlockSpec((ROW_TILE, d), lambda i: (i, 0)), _whole(wh), _whole(wl), _whole(tri)],
        out_specs=[pl.BlockSpec((ROW_TILE, LANES), lambda i: (i, 0)),
                   pl.BlockSpec((1, LANES), lambda i: (0, 0))],
        out_shape=[jax.ShapeDtypeStruct((t, LANES), F32), jax.ShapeDtypeStruct((1, LANES), F32)],
        scratch_shapes=[pltpu.VMEM((1, LANES), F32)],
        compiler_params=_params("arbitrary"),
        name="router",
    )(x2, wh, wl, tri)


def _row_copy(src_ref, src_row, dst_ref, dst_row, sem):
    return pltpu.make_async_copy(src_ref.at[pl.ds(src_row, 1), :], dst_ref.at[pl.ds(dst_row, 1), :], sem)


def _issue_row_copies(n_tokens, copy_of):
    def group(gidx, _):
        for u in range(ISSUE_UNROLL):
            for k in range(TOP_K):
                copy_of(gidx * ISSUE_UNROLL + u, k).start(priority=k)
        return 0

    lax.fori_loop(0, n_tokens // ISSUE_UNROLL, group, 0)


def _dispatch_body(pad_end_ref, dest_ref, x_ref, xg_ref, zero_ref, sem, zero_sem):
    tm = x_ref.shape[0]

    @pl.when(pl.program_id(0) == 0)
    def _():
        zero_ref[...] = jnp.zeros_like(zero_ref)

        def fill(start):
            start = pl.multiple_of(start, EXPERT_BLOCK)
            copy = pltpu.make_async_copy(zero_ref, xg_ref.at[pl.ds(start, EXPERT_BLOCK), :], zero_sem)
            copy.start()
            copy.wait()

        for e in range(N_EXPERTS):
            fill(jnp.maximum(pad_end_ref[e] - EXPERT_BLOCK, 0))
            tail = pad_end_ref[N_EXPERTS - 1] + e * EXPERT_BLOCK
            pl.when(tail < xg_ref.shape[0])(functools.partial(fill, tail))

    _issue_row_copies(tm, lambda tok, k: _row_copy(x_ref, tok, xg_ref, dest_ref[0, 0, TOP_K * tok + k], sem))
    for _ in range(TOP_K):
        pltpu.make_async_copy(x_ref, xg_ref.at[pl.ds(0, tm), :], sem).wait()


def _dispatch(x2, dest, pad_end, n_rows):
    t, d = x2.shape
    n_tiles = t // ROW_TILE
    grid_spec = pltpu.PrefetchScalarGridSpec(
        num_scalar_prefetch=1,
        grid=(n_tiles,),
        in_specs=[pl.BlockSpec((1, 1, TOP_K * ROW_TILE), lambda i, pe: (i, 0, 0), memory_space=pltpu.SMEM),
                  pl.BlockSpec((ROW_TILE, d), lambda i, pe: (i, 0))],
        out_specs=pl.BlockSpec(memory_space=pl.ANY),
        scratch_shapes=[pltpu.VMEM((EXPERT_BLOCK, d), F32), pltpu.SemaphoreType.DMA(()),
                        pltpu.SemaphoreType.DMA(())],
    )
    return pl.pallas_call(
        _dispatch_body,
        grid_spec=grid_spec,
        out_shape=jax.ShapeDtypeStruct((n_rows, d), F32),
        compiler_params=_params("arbitrary"),
        name="dispatch",
    )(pad_end, dest.reshape(n_tiles, 1, TOP_K * ROW_TILE), x2)


def _experts_body(be_ref, nused_ref, x_ref, wg_ref, wu_ref, wd_ref, o_ref):
    blk = pl.program_id(0)
    j = pl.program_id(1)

    @pl.when(blk < nused_ref[0])
    def _():
        y = _swiglu(x_ref[...].astype(BF16), wg_ref[...], wu_ref[...], wd_ref[...])

        @pl.when(j == 0)
        def _():
            o_ref[...] = y

        @pl.when(j > 0)
        def _():
            o_ref[...] += y

    @pl.when(blk >= nused_ref[0])
    def _():
        o_ref[...] = jnp.zeros_like(o_ref)


def _experts(layer, xg, block_e, n_used, wg, wu, wd):
    n_rows, d = xg.shape
    f = wg.shape[-1]
    ft = f // EXPERT_F_TILES
    n_blocks = n_rows // EXPERT_BLOCK
    last = EXPERT_F_TILES - 1

    def used(blk, nu):
        return jnp.maximum(jnp.minimum(blk, nu[0] - 1), 0)

    def f_tile(blk, j, nu):
        return jnp.where(blk < nu[0], j, last)

    grid_spec = pltpu.PrefetchScalarGridSpec(
        num_scalar_prefetch=2,
        grid=(n_blocks, EXPERT_F_TILES),
        in_specs=[pl.BlockSpec((EXPERT_BLOCK, d), lambda blk, j, be, nu: (used(blk, nu), 0)),
                  pl.BlockSpec((None, None, d, ft),
                               lambda blk, j, be, nu: (layer, be[used(blk, nu)], 0, f_tile(blk, j, nu))),
                  pl.BlockSpec((None, None, d, ft),
                               lambda blk, j, be, nu: (layer, be[used(blk, nu)], 0, f_tile(blk, j, nu))),
                  pl.BlockSpec((None, None, ft, d),
                               lambda blk, j, be, nu: (layer, be[used(blk, nu)], f_tile(blk, j, nu), 0))],
        out_specs=pl.BlockSpec((EXPERT_BLOCK, d), lambda blk, j, be, nu: (blk, 0)),
    )
    return pl.pallas_call(
        _experts_body,
        grid_spec=grid_spec,
        out_shape=jax.ShapeDtypeStruct((n_rows, d), F32),
        compiler_params=_params("arbitrary", "arbitrary"),
        name="experts",
    )(block_e, n_used, xg, wg, wu, wd)


def _combine_ln_body(dest_ref, dest_next_ref, x_ref, w_ref, yb_ref, g_ref, b_ref, o_ref, buf_ref, sem):
    i = pl.program_id(0)
    tm = x_ref.shape[0]
    slot = lax.rem(i, 2)

    def gather(idx_ref, into):
        _issue_row_copies(tm, lambda tok, k: _row_copy(yb_ref, idx_ref[0, 0, TOP_K * tok + k],
                                                       buf_ref.at[into, k], tok, sem.at[into]))

    @pl.when(i == 0)
    def _():
        gather(dest_ref, 0)

    @pl.when(i + 1 < pl.num_programs(0))
    def _():
        gather(dest_next_ref, 1 - slot)

    for k in range(TOP_K):
        pltpu.make_async_copy(yb_ref.at[pl.ds(0, tm), :], buf_ref.at[slot, k], sem.at[slot]).wait()
    w = w_ref[...]
    y = w[:, _COL_W0:_COL_W0 + 1] * buf_ref[slot, 0] + w[:, _COL_W1:_COL_W1 + 1] * buf_ref[slot, 1]
    o_ref[...] = _layer_norm(ALPHA * x_ref[...] + y, g_ref[...], b_ref[...])


def _combine_ln(x2, meta, yb, dest, g, b):
    t, d = x2.shape
    n_tiles = t // ROW_TILE
    dest = dest.reshape(n_tiles, 1, TOP_K * ROW_TILE)
    dest_block = lambda index_map: pl.BlockSpec((1, 1, TOP_K * ROW_TILE), index_map, memory_space=pltpu.SMEM)
    return pl.pallas_call(
        _combine_ln_body,
        grid=(n_tiles,),
        in_specs=[dest_block(lambda i: (i, 0, 0)),
                  dest_block(lambda i: (jnp.minimum(i + 1, n_tiles - 1), 0, 0)),
                  pl.BlockSpec((ROW_TILE, d), lambda i: (i, 0)),
                  pl.BlockSpec((ROW_TILE, LANES), lambda i: (i, 0)),
                  pl.BlockSpec(memory_space=pl.ANY),
                  _whole(g), _whole(b)],
        out_specs=pl.BlockSpec((ROW_TILE, d), lambda i: (i, 0)),
        out_shape=jax.ShapeDtypeStruct((t, d), F32),
        scratch_shapes=[pltpu.VMEM((2, TOP_K, ROW_TILE, d), F32), pltpu.SemaphoreType.DMA((2,))],
        compiler_params=_params("arbitrary"),
        name="combine_ln",
    )(dest, dest, x2, meta, yb, g, b)


def _moe_ln(layer, x2, w_router, wg, wu, wd, g, b):
    t, _ = x2.shape
    meta, counts = _router(x2, w_router)
    counts = counts[0, :N_EXPERTS].astype(jnp.int32)
    padded = (counts + EXPERT_BLOCK - 1) // EXPERT_BLOCK * EXPERT_BLOCK
    pad_end = jnp.cumsum(padded)
    pad_start = pad_end - padded
    experts = meta[:, _COL_E0:_COL_E1 + 1].astype(jnp.int32)
    ranks = meta[:, _COL_R0:_COL_R1 + 1].astype(jnp.int32)
    dest = pad_start[experts] + ranks
    n_blocks = t * TOP_K // EXPERT_BLOCK + N_EXPERTS
    block_start = jnp.arange(n_blocks, dtype=jnp.int32) * EXPERT_BLOCK
    block_e = jnp.minimum(jnp.sum(pad_end[None, :] <= block_start[:, None], axis=1), N_EXPERTS - 1).astype(jnp.int32)
    n_used = (pad_end[-1:] // EXPERT_BLOCK).astype(jnp.int32)
    xg = _dispatch(x2, dest, pad_end.astype(jnp.int32), n_blocks * EXPERT_BLOCK)
    yb = _experts(layer, xg, block_e, n_used, wg, wu, wd)
    return _combine_ln(x2, meta, yb, dest, g, b)


def _block_diag(w):
    n_g, c, _ = w.shape
    out = jnp.zeros((n_g * c, n_g * c), w.dtype)
    for gi in range(n_g):
        out = out.at[gi * c:(gi + 1) * c, gi * c:(gi + 1) * c].set(w[gi])
    return out


def kernel(x, mem, w_mem_kv, a_w_in, a_conv_w, a_w_out, b_w_in, b_w_out, c_w_in, c_pool_w, c_pool_scale, c_w_out, ln_g, ln_b, ffn_w_gate, ffn_w_up, ffn_w_down, moe_router, moe_w_gate, moe_w_up, moe_w_down):
    bsz, seq_len, d = x.shape
    t = bsz * seq_len
    kt, v = _mem_kv(mem, w_mem_kv.astype(BF16))
    x = x.astype(F32)
    a_w_in, a_w_out, b_w_in, b_w_out, c_w_in, c_w_out = (
        w.astype(BF16) for w in (a_w_in, a_w_out, b_w_in, b_w_out, c_w_in, c_w_out))
    ffn_w = [w.astype(BF16) for w in (ffn_w_gate, ffn_w_up, ffn_w_down)]
    moe_w = [w.astype(BF16) for w in (moe_w_gate, moe_w_up, moe_w_down)]
    for i in range(DEPTH):
        kind, j = i % 3, i // 3
        g0, b0 = ln_g[i, 0].reshape(1, d), ln_b[i, 0].reshape(1, d)
        g1, b1 = ln_g[i, 1].reshape(1, d), ln_b[i, 1].reshape(1, d)
        if kind == 0:
            x = _mixer_layer(_conv_layer_body, j, x, a_w_in, [a_conv_w[j]], kt, v, a_w_out, g0, b0)
        elif kind == 1:
            h = _proj_in(j, x.reshape(t, d), b_w_in).reshape(bsz, seq_len, -1)
            a = _attention(h, kt, v)
            x = _proj_out_ln(j, a.reshape(t, -1), x.reshape(t, d), b_w_out, g0, b0)
            x = x.reshape(bsz, seq_len, d)
        else:
            x = _mixer_layer(_pool_layer_body, j, x, c_w_in,
                             [_block_diag(c_pool_w[j]).astype(BF16), c_pool_scale[j].reshape(1, -1)], kt, v,
                             c_w_out, g0, b0)
        f = i // 2
        x2 = x.reshape(t, d)
        if i % 2 == 0:
            x2 = _ffn_ln(f, x2, *ffn_w, g1, b1)
        else:
            x2 = _moe_ln(f, x2, moe_router[f], *moe_w, g1, b1)
        x = x2.reshape(bsz, seq_len, d)
    return x
```

```python
import functools

import jax
import jax.numpy as jnp
from jax import lax
from jax.experimental import pallas as pl
from jax.experimental.pallas import tpu as pltpu

F32 = jnp.float32
BF16 = jnp.bfloat16

D_MODEL = 1024
DEPTH = 4
HEAD_DIM = 64
MEM_HEADS = 4
MEM_WIDTH = MEM_HEADS * HEAD_DIM
MIX_WIDTH = 3 * D_MODEL // 4
DIL_GROUPS = ((128, 1), (512, 4), (2048, 16))
DIL_HEADS = 4
DIL_WIDTH = DIL_HEADS * HEAD_DIM
DIL_RADIUS = 64
POOL_WINDOWS = (2, 4, 8, 16)
POOL_GROUP = MIX_WIDTH // len(POOL_WINDOWS)
N_EXPERTS = 8
TOP_K = 2
LN_EPS = 1e-5
NEG_INF = -1e30
ALPHA = (2 * DEPTH) ** 0.25

LANES = 128
SUBLANES = 8
HALO = SUBLANES
SEQ_TILE = 512
ROW_TILE = 512
DISPATCH_TILE = 1024
EXPERT_BLOCK = 512
EXPERT_F_TILES = 2
Q_BLOCK = 128
ISSUE_UNROLL = 8
VMEM_LIMIT = 56 * 1024 * 1024


def _params(*sem):
    return pltpu.CompilerParams(dimension_semantics=sem, vmem_limit_bytes=VMEM_LIMIT)


def _layer_norm(v, g, b):
    mu = jnp.mean(v, axis=-1, keepdims=True)
    d = v - mu
    var = jnp.mean(d * d, axis=-1, keepdims=True)
    return d * lax.rsqrt(var + LN_EPS) * g + b


def _dot(a, b):
    return jnp.dot(a, b, preferred_element_type=F32)


def _head_masks(width):
    col = lax.broadcasted_iota(jnp.int32, (1, width), 1)
    return [(col >= h * HEAD_DIM) & (col < (h + 1) * HEAD_DIM) for h in range(width // HEAD_DIM)]


def _stack_heads(q, masks):
    return jnp.concatenate([jnp.where(m, q, 0.0) for m in masks], axis=0).astype(BF16)


def _unstack_heads(o, masks, n):
    out = o[(len(masks) - 1) * n:]
    for h in range(len(masks) - 2, -1, -1):
        out = jnp.where(masks[h], o[h * n:(h + 1) * n], out)
    return out


def _mem_attention(q, kt, v):
    n = q.shape[0]
    masks = _head_masks(MEM_WIDTH)
    sc = _dot(_stack_heads(q, masks), kt) * HEAD_DIM ** -0.5
    p = jnp.exp(sc - jnp.max(sc, axis=-1, keepdims=True))
    p = p / jnp.sum(p, axis=-1, keepdims=True)
    return _unstack_heads(_dot(p.astype(BF16), v), masks, n)


def _mem_kv_body(mem_ref, w_ref, kt_ref, v_ref):
    kv = _dot(mem_ref[...].astype(BF16), w_ref[...])
    kt_ref[...] = kv[:, :MEM_WIDTH].T.astype(BF16)
    v_ref[...] = kv[:, MEM_WIDTH:].astype(BF16)


def _mem_kv(mem, w_kv):
    b, m, d = mem.shape
    return pl.pallas_call(
        _mem_kv_body,
        grid=(b,),
        in_specs=[pl.BlockSpec((None, m, d), lambda i: (i, 0, 0)),
                  pl.BlockSpec((d, 2 * MEM_WIDTH), lambda i: (0, 0))],
        out_specs=[pl.BlockSpec((None, MEM_WIDTH, m), lambda i: (i, 0, 0)),
                   pl.BlockSpec((None, m, MEM_WIDTH), lambda i: (i, 0, 0))],
        out_shape=[jax.ShapeDtypeStruct((b, MEM_WIDTH, m), BF16),
                   jax.ShapeDtypeStruct((b, m, MEM_WIDTH), BF16)],
        compiler_params=_params("arbitrary"),
        name="mem_kv",
    )(mem, w_kv)


def _ext_rows(xp_ref, xm_ref, xn_ref, seq_len):
    s = pl.program_id(1)
    ts = xm_ref.shape[0]
    xm = xm_ref[...]
    xe = jnp.concatenate([xp_ref[...], xm, xn_ref[...]], axis=0).astype(BF16)
    pos = s * ts - HALO + lax.broadcasted_iota(jnp.int32, (ts + 2 * HALO, 1), 0)
    valid = (pos >= 0) & (pos < seq_len)
    return xm, xe, valid


def _shift_rows(a, k):
    n = a.shape[0]
    return pltpu.roll(a, k % n, 0)


def _mixer_tail(mix, q_mem, xm, kt_ref, v_ref, wout_ref, g_ref, b_ref, o_ref):
    mem_out = _mem_attention(q_mem, kt_ref[...], v_ref[...])
    y = (_dot(mix.astype(BF16), wout_ref[:MIX_WIDTH, :])
         + _dot(mem_out.astype(BF16), wout_ref[MIX_WIDTH:, :]))
    o_ref[...] = _layer_norm(ALPHA * xm + y, g_ref[...], b_ref[...])


def _conv_layer_body(seq_len, xp_ref, xm_ref, xn_ref, win_ref, cw_ref, kt_ref, v_ref,
                     wout_ref, g_ref, b_ref, o_ref):
    ts = xm_ref.shape[0]
    xm, xe, valid = _ext_rows(xp_ref, xm_ref, xn_ref, seq_len)
    xmb = xm.astype(BF16)
    gate_b = _dot(xmb, win_ref[:, :MIX_WIDTH])
    cu = _dot(xe, win_ref[:, MIX_WIDTH:3 * MIX_WIDTH])
    q_mem = _dot(xmb, win_ref[:, 3 * MIX_WIDTH:])
    z = jnp.where(valid, cu[:, :MIX_WIDTH] * cu[:, MIX_WIDTH:], 0.0)
    cw = cw_ref[...]
    conv = (cw[0:1] * _shift_rows(z, 1)[HALO:HALO + ts]
            + cw[1:2] * z[HALO:HALO + ts]
            + cw[2:3] * _shift_rows(z, -1)[HALO:HALO + ts])
    _mixer_tail(gate_b * conv, q_mem, xm, kt_ref, v_ref, wout_ref, g_ref, b_ref, o_ref)


def _pool_layer_body(seq_len, xp_ref, xm_ref, xn_ref, win_ref, pw_ref, ps_ref, kt_ref, v_ref,
                     wout_ref, g_ref, b_ref, o_ref):
    ts = xm_ref.shape[0]
    s = pl.program_id(1)
    xm, xe, valid = _ext_rows(xp_ref, xm_ref, xn_ref, seq_len)
    u = jnp.where(valid, _dot(xe, win_ref[:, :MIX_WIDTH]), 0.0)
    q_mem = _dot(xm.astype(BF16), win_ref[:, MIX_WIDTH:])
    a2 = u + _shift_rows(u, 1)
    a4 = _shift_rows(a2, 1) + _shift_rows(a2, -1)
    a8 = _shift_rows(a4, 2) + _shift_rows(a4, -2)
    a16 = _shift_rows(a8, 4) + _shift_rows(a8, -4)
    col = lax.broadcasted_iota(jnp.int32, (1, MIX_WIDTH), 1)
    pos = s * ts + lax.broadcasted_iota(jnp.int32, (ts, 1), 0)
    num = a16[HALO:HALO + ts]
    cnt = None
    for gi in range(len(POOL_WINDOWS) - 1, -1, -1):
        w = POOL_WINDOWS[gi]
        c_w = (jnp.minimum(pos + (w // 2 - 1), seq_len - 1) - jnp.maximum(pos - w // 2, 0) + 1).astype(F32)
        if cnt is None:
            cnt = jnp.broadcast_to(c_w, (ts, MIX_WIDTH))
        else:
            in_group = col < (gi + 1) * POOL_GROUP
            num = jnp.where(in_group, (a2, a4, a8)[gi][HALO:HALO + ts], num)
            cnt = jnp.where(in_group, c_w, cnt)
    diff = num / cnt - u[HALO:HALO + ts]
    mix = _dot(diff.astype(BF16), pw_ref[...]) * ps_ref[...]
    _mixer_tail(mix, q_mem, xm, kt_ref, v_ref, wout_ref, g_ref, b_ref, o_ref)


def _whole(w, layer=None):
    if layer is None:
        return pl.BlockSpec(w.shape, lambda *_: (0,) * w.ndim)
    return pl.BlockSpec((None,) + w.shape[1:], lambda *_: (layer,) + (0,) * (w.ndim - 1))


def _mixer_layer(body, layer, x, w_in, extra, kt, v, w_out, g, b):
    bsz, seq_len, d = x.shape
    ts = SEQ_TILE
    n_halo_blocks = seq_len // HALO
    in_specs = [
        pl.BlockSpec((None, HALO, d), lambda i, s: (i, jnp.maximum(s * (ts // HALO) - 1, 0), 0)),
        pl.BlockSpec((None, ts, d), lambda i, s: (i, s, 0)),
        pl.BlockSpec((None, HALO, d), lambda i, s: (i, jnp.minimum((s + 1) * (ts // HALO), n_halo_blocks - 1), 0)),
        _whole(w_in, layer),
    ]
    in_specs += [_whole(e) for e in extra]
    in_specs += [
        pl.BlockSpec((None,) + kt.shape[1:], lambda i, s: (i, 0, 0)),
        pl.BlockSpec((None,) + v.shape[1:], lambda i, s: (i, 0, 0)),
        _whole(w_out, layer),
        _whole(g),
        _whole(b),
    ]
    return pl.pallas_call(
        functools.partial(body, seq_len),
        grid=(bsz, seq_len // ts),
        in_specs=in_specs,
        out_specs=pl.BlockSpec((None, ts, d), lambda i, s: (i, s, 0)),
        out_shape=jax.ShapeDtypeStruct(x.shape, F32),
        compiler_params=_params("arbitrary", "arbitrary"),
        name=body.__name__.strip("_"),
    )(x, x, x, w_in, *extra, kt, v, w_out, g, b)


def _proj_in_body(x_ref, w_ref, o_ref):
    o_ref[...] = _dot(x_ref[...].astype(BF16), w_ref[...])


def _proj_in(layer, x2, w):
    t, d = x2.shape
    n = w.shape[-1]
    return pl.pallas_call(
        _proj_in_body,
        grid=(t // ROW_TILE,),
        in_specs=[pl.BlockSpec((ROW_TILE, d), lambda i: (i, 0)),
                  _whole(w, layer)],
        out_specs=pl.BlockSpec((ROW_TILE, n), lambda i: (i, 0)),
        out_shape=jax.ShapeDtypeStruct((t, n), F32),
        compiler_params=_params("arbitrary"),
        name="proj_in",
    )(x2, w)


def _alibi_slope(index, total):
    return 2.0 ** (-8.0 * (index + 1) / total)


def _rows(start, size, stride):
    return pl.ds(start, size) if stride == 1 else pl.ds(start, size, stride=stride)


def _load_cols(refs, rows):
    return jnp.concatenate([r[rows, :] for r in refs], axis=1)


def _store_cols(refs, rows, val):
    for c, r in enumerate(refs):
        r[rows, :] = val[:, c * LANES:(c + 1) * LANES]


def _dilated_group(gi, dilation, seq_len, hq_refs, hk_refs, hv_refs, m_refs, l_refs, o_refs):
    n_sub = seq_len // dilation
    qb = min(Q_BLOCK, n_sub)
    kw = min(qb + 2 * DIL_RADIUS, n_sub)
    masks = _head_masks(DIL_WIDTH)
    n_heads_total = len(DIL_GROUPS) * DIL_HEADS
    bias_cache = {}

    def bias_and_mask(offset):
        if offset not in bias_cache:
            rel = offset + lax.broadcasted_iota(jnp.int32, (qb, kw), 0) - lax.broadcasted_iota(jnp.int32, (qb, kw), 1)
            dist = jnp.abs(rel)
            inside = dist <= DIL_RADIUS
            span = (dist * dilation).astype(F32)
            bias = jnp.concatenate(
                [-_alibi_slope(gi * DIL_HEADS + h, n_heads_total) * span for h in range(DIL_HEADS)], axis=0)
            bias_cache[offset] = (bias, jnp.concatenate([inside] * DIL_HEADS, axis=0))
        return bias_cache[offset]

    for r in range(dilation):
        for j0 in range(0, n_sub, qb):
            ks = min(max(j0 - DIL_RADIUS, 0), n_sub - kw)
            q_rows = _rows(r + j0 * dilation, qb, dilation)
            k_rows = _rows(r + ks * dilation, kw, dilation)
            q = _load_cols(hq_refs, q_rows)
            k = _load_cols(hk_refs, k_rows).astype(BF16)
            v = _load_cols(hv_refs, k_rows).astype(BF16)
            sc = lax.dot_general(_stack_heads(q, masks), k, (((1,), (1,)), ((), ())),
                                 preferred_element_type=F32) * HEAD_DIM ** -0.5
            bias, inside = bias_and_mask(j0 - ks)
            sc = jnp.where(inside, sc + bias, NEG_INF)
            mx = jnp.max(sc, axis=-1, keepdims=True)
            p = jnp.exp(sc - mx)
            den = jnp.sum(p, axis=-1, keepdims=True)
            num = _unstack_heads(_dot(p.astype(BF16), v), masks, qb)
            mx = _unstack_heads(jnp.broadcast_to(mx, (DIL_HEADS * qb, DIL_WIDTH)), masks, qb)
            den = _unstack_heads(jnp.broadcast_to(den, (DIL_HEADS * qb, DIL_WIDTH)), masks, qb)
            if gi > 0:
                m_old = _load_cols(m_refs, q_rows)
                m_new = jnp.maximum(m_old, mx)
                a_old = jnp.exp(m_old - m_new)
                a_new = jnp.exp(mx - m_new)
                mx = m_new
                den = _load_cols(l_refs, q_rows) * a_old + den * a_new
                num = _load_cols(o_refs, q_rows) * a_old + num * a_new
            _store_cols(m_refs, q_rows, mx)
            _store_cols(l_refs, q_rows, den)
            _store_cols(o_refs, q_rows, num)


def _attention_body(*refs):
    n_slab = DIL_WIDTH // LANES
    hq_refs, hk_refs, hv_refs = (refs[i * n_slab:(i + 1) * n_slab] for i in range(3))
    hm_ref, kt_ref, v_ref, out_ref = refs[3 * n_slab:3 * n_slab + 4]
    m_refs, l_refs, o_refs = (refs[3 * n_slab + 4 + i * n_slab:3 * n_slab + 4 + (i + 1) * n_slab] for i in range(3))
    g = pl.program_id(1)
    seq_len = hm_ref.shape[0]

    @pl.when(g == 0)
    def _():
        for c in range(0, seq_len, ROW_TILE):
            mem_out = _mem_attention(hm_ref[c:c + ROW_TILE, :], kt_ref[...], v_ref[...])
            out_ref[c:c + ROW_TILE, DIL_WIDTH:] = mem_out.astype(BF16)

    for gi, (_, dilation) in enumerate(DIL_GROUPS):
        @pl.when(g == gi)
        def _(gi=gi, dilation=dilation):
            _dilated_group(gi, dilation, seq_len, hq_refs, hk_refs, hv_refs, m_refs, l_refs, o_refs)

    @pl.when(g == len(DIL_GROUPS) - 1)
    def _():
        for c in range(n_slab):
            out_ref[:, c * LANES:(c + 1) * LANES] = (o_refs[c][...] / l_refs[c][...]).astype(BF16)


def _attention(h, kt, v):
    bsz, seq_len, _ = h.shape
    n_g = len(DIL_GROUPS)
    n_slab = DIL_WIDTH // LANES

    def slabs(base):
        return [pl.BlockSpec((None, seq_len, LANES), lambda i, g, c=c: (i, 0, (base + g) * n_slab + c))
                for c in range(n_slab)]

    return pl.pallas_call(
        _attention_body,
        grid=(bsz, n_g),
        in_specs=slabs(0) + slabs(n_g) + slabs(2 * n_g) + [
            pl.BlockSpec((None, seq_len, MEM_WIDTH), lambda i, g: (i, 0, 3 * n_g)),
            pl.BlockSpec((None,) + kt.shape[1:], lambda i, g: (i, 0, 0)),
            pl.BlockSpec((None,) + v.shape[1:], lambda i, g: (i, 0, 0))],
        out_specs=pl.BlockSpec((None, seq_len, DIL_WIDTH + MEM_WIDTH), lambda i, g: (i, 0, 0)),
        out_shape=jax.ShapeDtypeStruct((bsz, seq_len, DIL_WIDTH + MEM_WIDTH), BF16),
        scratch_shapes=[pltpu.VMEM((seq_len, LANES), F32)] * (3 * n_slab),
        compiler_params=_params("arbitrary", "arbitrary"),
        name="dilated_attention",
    )(*([h] * (3 * n_slab + 1)), kt, v)


def _proj_out_ln_body(a_ref, x_ref, w_ref, g_ref, b_ref, o_ref):
    y = _dot(a_ref[...], w_ref[...])
    o_ref[...] = _layer_norm(ALPHA * x_ref[...] + y, g_ref[...], b_ref[...])


def _proj_out_ln(layer, a2, x2, w, g, b):
    t, d = x2.shape
    k = a2.shape[1]
    return pl.pallas_call(
        _proj_out_ln_body,
        grid=(t // ROW_TILE,),
        in_specs=[pl.BlockSpec((ROW_TILE, k), lambda i: (i, 0)),
                  pl.BlockSpec((ROW_TILE, d), lambda i: (i, 0)),
                  _whole(w, layer), _whole(g), _whole(b)],
        out_specs=pl.BlockSpec((ROW_TILE, d), lambda i: (i, 0)),
        out_shape=jax.ShapeDtypeStruct((t, d), F32),
        compiler_params=_params("arbitrary"),
        name="proj_out_ln",
    )(a2, x2, w, g, b)


def _swiglu(xb, wg, wu, wd):
    g = _dot(xb, wg)
    u = _dot(xb, wu)
    return _dot((g * jax.nn.sigmoid(g) * u).astype(BF16), wd)


def _ffn_ln_body(f_chunk, x_ref, wg_ref, wu_ref, wd_ref, g_ref, b_ref, o_ref):
    x = x_ref[...]
    xb = x.astype(BF16)
    y = None
    for c in range(0, wg_ref.shape[1], f_chunk):
        part = _swiglu(xb, wg_ref[:, c:c + f_chunk], wu_ref[:, c:c + f_chunk], wd_ref[c:c + f_chunk, :])
        y = part if y is None else y + part
    o_ref[...] = _layer_norm(ALPHA * x + y, g_ref[...], b_ref[...])


def _ffn_ln(layer, x2, wg, wu, wd, g, b):
    t, d = x2.shape
    f = wg.shape[-1]
    resident = lambda w: pl.BlockSpec((None,) + w.shape[1:], lambda i: (layer, 0, 0), pipeline_mode=pl.Buffered(1))
    return pl.pallas_call(
        functools.partial(_ffn_ln_body, f // 2),
        grid=(t // ROW_TILE,),
        in_specs=[pl.BlockSpec((ROW_TILE, d), lambda i: (i, 0)),
                  resident(wg), resident(wu), resident(wd), _whole(g), _whole(b)],
        out_specs=pl.BlockSpec((ROW_TILE, d), lambda i: (i, 0)),
        out_shape=jax.ShapeDtypeStruct((t, d), F32),
        compiler_params=_params("arbitrary"),
        name="ffn_ln",
    )(x2, wg, wu, wd, g, b)


_COL_E0, _COL_E1, _COL_R0, _COL_R1, _COL_W0, _COL_W1 = range(6)


def _split_bf16(a):
    hi = a.astype(BF16)
    return hi, (a - hi.astype(F32)).astype(BF16)


def _router_body(x_ref, whl_ref, tri_ref, meta_ref, cnt_ref, carry_ref):
    i = pl.program_id(0)

    @pl.when(i == 0)
    def _():
        carry_ref[...] = jnp.zeros_like(carry_ref)

    tm = x_ref.shape[0]
    lane = lax.broadcasted_iota(jnp.int32, (tm, LANES), 1).astype(F32)
    xh, xl = _split_bf16(x_ref[...])
    both = _dot(xh, whl_ref[...])
    logits = both[:, :LANES] + (_dot(xl, whl_ref[:, :LANES]) + both[:, LANES:])
    logits = jnp.where(lane < N_EXPERTS, logits, -jnp.inf)
    m0 = jnp.max(logits, axis=-1, keepdims=True)
    e0 = jnp.min(jnp.where(logits == m0, lane, float(LANES)), axis=-1, keepdims=True)
    rest = jnp.where(lane == e0, -jnp.inf, logits)
    m1 = jnp.max(rest, axis=-1, keepdims=True)
    e1 = jnp.min(jnp.where(rest == m1, lane, float(LANES)), axis=-1, keepdims=True)
    ex = jnp.exp(m1 - m0)
    w0 = 1.0 / (1.0 + ex)
    w1 = ex / (1.0 + ex)
    hit0 = lane == e0
    hit1 = lane == e1
    onehot = (hit0 | hit1).astype(F32)
    before = _dot(tri_ref[...], onehot.astype(BF16)) + carry_ref[...]
    r0 = jnp.sum(jnp.where(hit0, before, 0.0), axis=-1, keepdims=True)
    r1 = jnp.sum(jnp.where(hit1, before, 0.0), axis=-1, keepdims=True)
    carry_ref[...] += jnp.sum(onehot, axis=0, keepdims=True)
    cnt_ref[...] = carry_ref[...]
    meta = jnp.zeros((tm, LANES), F32)
    for col, val in ((_COL_E0, e0), (_COL_E1, e1), (_COL_R0, r0), (_COL_R1, r1), (_COL_W0, w0), (_COL_W1, w1)):
        meta = jnp.where(lane == col, val, meta)
    meta_ref[...] = meta


def _router(x2, w_router):
    t, d = x2.shape
    whl = jnp.concatenate(_split_bf16(jnp.zeros((d, LANES), F32).at[:, :N_EXPERTS].set(w_router)), axis=1)
    tri = jnp.tril(jnp.ones((ROW_TILE, ROW_TILE), BF16), -1)
    return pl.pallas_call(
        _router_body,
        grid=(t // ROW_TILE,),
        in_specs=[pl.BlockSpec((ROW_TILE, d), lambda i: (i, 0)), _whole(whl), _whole(tri)],
        out_specs=[pl.BlockSpec((ROW_TILE, LANES), lambda i: (i, 0)),
                   pl.BlockSpec((1, LANES), lambda i: (0, 0))],
        out_shape=[jax.ShapeDtypeStruct((t, LANES), F32), jax.ShapeDtypeStruct((1, LANES), F32)],
        scratch_shapes=[pltpu.VMEM((1, LANES), F32)],
        compiler_params=_params("arbitrary"),
        name="router",
    )(x2, whl, tri)


assert D_MODEL == SUBLANES * LANES


def _store_row_tiles(ref, val):
    n = val.shape[0]
    for c in range(SUBLANES):
        ref[pl.ds(c, n, stride=SUBLANES), :] = val[:, c * LANES:(c + 1) * LANES]


def _load_row_tiles(ref, n):
    return jnp.concatenate([ref[pl.ds(c, n, stride=SUBLANES), :] for c in range(SUBLANES)], axis=1)


def _tile_of(ref, row):
    return ref.at[pl.ds(pl.multiple_of(row * SUBLANES, SUBLANES), SUBLANES), :]


def _wait_rows(hbm_rows_ref, n, sem):
    span = hbm_rows_ref.at[pl.ds(0, n)]
    pltpu.make_async_copy(span, span, sem).wait()


def _issue_row_copies(n_tokens, copy_of):
    def group(gidx, _):
        for u in range(ISSUE_UNROLL):
            for k in range(TOP_K):
                copy_of(gidx * ISSUE_UNROLL + u, k).start(priority=k)
        return 0

    lax.fori_loop(0, n_tokens // ISSUE_UNROLL, group, 0)


def _dispatch_body(pad_end_ref, dest_ref, x_ref, xg_ref, stage_ref, zero_ref, sem, zero_sem):
    tm = x_ref.shape[0]

    @pl.when(pl.program_id(0) == 0)
    def _():
        zero_ref[...] = jnp.zeros_like(zero_ref)

        def fill(start):
            copy = pltpu.make_async_copy(zero_ref, xg_ref.at[pl.ds(start, EXPERT_BLOCK)], zero_sem)
            copy.start()
            copy.wait()

        for e in range(N_EXPERTS):
            fill(jnp.maximum(pad_end_ref[e] - EXPERT_BLOCK, 0))
            tail = pad_end_ref[N_EXPERTS - 1] + e * EXPERT_BLOCK
            pl.when(tail < xg_ref.shape[0])(functools.partial(fill, tail))

    _store_row_tiles(stage_ref, x_ref[...])
    _issue_row_copies(tm, lambda tok, k: pltpu.make_async_copy(
        _tile_of(stage_ref, tok), xg_ref.at[dest_ref[0, 0, TOP_K * tok + k]], sem))
    for _ in range(TOP_K):
        _wait_rows(xg_ref, tm, sem)


def _dispatch(x2, dest, pad_end, n_rows):
    t, d = x2.shape
    tm = DISPATCH_TILE
    n_tiles = t // tm
    grid_spec = pltpu.PrefetchScalarGridSpec(
        num_scalar_prefetch=1,
        grid=(n_tiles,),
        in_specs=[pl.BlockSpec((1, 1, TOP_K * tm), lambda i, pe: (i, 0, 0), memory_space=pltpu.SMEM),
                  pl.BlockSpec((tm, d), lambda i, pe: (i, 0))],
        out_specs=pl.BlockSpec(memory_space=pl.ANY),
        scratch_shapes=[pltpu.VMEM((tm * SUBLANES, LANES), F32),
                        pltpu.VMEM((EXPERT_BLOCK, SUBLANES, LANES), F32),
                        pltpu.SemaphoreType.DMA(()), pltpu.SemaphoreType.DMA(())],
    )
    return pl.pallas_call(
        _dispatch_body,
        grid_spec=grid_spec,
        out_shape=jax.ShapeDtypeStruct((n_rows, SUBLANES, LANES), F32),
        compiler_params=_params("arbitrary"),
        name="dispatch",
    )(pad_end, dest.reshape(n_tiles, 1, TOP_K * tm), x2)


def _experts_body(be_ref, nused_ref, x_ref, wg_ref, wu_ref, wd_ref, o_ref, acc_ref):
    blk = pl.program_id(0)
    j = pl.program_id(1)

    @pl.when(blk < nused_ref[0])
    def _():
        xb = _load_row_tiles(x_ref, EXPERT_BLOCK).astype(BF16)
        y = _swiglu(xb, wg_ref[...], wu_ref[...], wd_ref[...])

        @pl.when(j == 0)
        def _():
            acc_ref[...] = y

        @pl.when(j == EXPERT_F_TILES - 1)
        def _():
            _store_row_tiles(o_ref, acc_ref[...] + y)

    @pl.when(blk >= nused_ref[0])
    def _():
        o_ref[...] = jnp.zeros_like(o_ref)


def _experts(layer, xg, block_e, n_used, wg, wu, wd):
    assert EXPERT_F_TILES == 2
    n_rows = xg.shape[0] // SUBLANES
    d = D_MODEL
    f = wg.shape[-1]
    ft = f // EXPERT_F_TILES
    n_blocks = n_rows // EXPERT_BLOCK
    last = EXPERT_F_TILES - 1
    tile_rows = EXPERT_BLOCK * SUBLANES

    def used(blk, nu):
        return jnp.maximum(jnp.minimum(blk, nu[0] - 1), 0)

    def f_tile(blk, j, nu):
        return jnp.where(blk < nu[0], j, last)

    grid_spec = pltpu.PrefetchScalarGridSpec(
        num_scalar_prefetch=2,
        grid=(n_blocks, EXPERT_F_TILES),
        in_specs=[pl.BlockSpec((tile_rows, LANES), lambda blk, j, be, nu: (used(blk, nu), 0)),
                  pl.BlockSpec((None, None, d, ft),
                               lambda blk, j, be, nu: (layer, be[used(blk, nu)], 0, f_tile(blk, j, nu))),
                  pl.BlockSpec((None, None, d, ft),
                               lambda blk, j, be, nu: (layer, be[used(blk, nu)], 0, f_tile(blk, j, nu))),
                  pl.BlockSpec((None, None, ft, d),
                               lambda blk, j, be, nu: (layer, be[used(blk, nu)], f_tile(blk, j, nu), 0))],
        out_specs=pl.BlockSpec((tile_rows, LANES), lambda blk, j, be, nu: (blk, 0)),
        scratch_shapes=[pltpu.VMEM((EXPERT_BLOCK, d), F32)],
    )
    return pl.pallas_call(
        _experts_body,
        grid_spec=grid_spec,
        out_shape=jax.ShapeDtypeStruct(xg.shape, F32),
        compiler_params=_params("arbitrary", "arbitrary"),
        name="experts",
    )(block_e, n_used, xg, wg, wu, wd)


def _combine_ln_body(dest_ref, dest_next_ref, x_ref, w_ref, yb_ref, g_ref, b_ref, o_ref, buf_ref, sem):
    i = pl.program_id(0)
    tm = x_ref.shape[0]
    slot = lax.rem(i, 2)

    def gather(idx_ref, into):
        _issue_row_copies(tm, lambda tok, k: pltpu.make_async_copy(
            yb_ref.at[idx_ref[0, 0, TOP_K * tok + k]], _tile_of(buf_ref.at[into, k], tok), sem.at[into]))

    @pl.when(i == 0)
    def _():
        gather(dest_ref, 0)

    @pl.when(i + 1 < pl.num_programs(0))
    def _():
        gather(dest_next_ref, 1 - slot)

    for _ in range(TOP_K):
        _wait_rows(yb_ref, tm, sem.at[slot])
    w = w_ref[...]
    y = (w[:, _COL_W0:_COL_W0 + 1] * _load_row_tiles(buf_ref.at[slot, 0], tm)
         + w[:, _COL_W1:_COL_W1 + 1] * _load_row_tiles(buf_ref.at[slot, 1], tm))
    o_ref[...] = _layer_norm(ALPHA * x_ref[...] + y, g_ref[...], b_ref[...])


def _combine_ln(x2, meta, yb, dest, g, b):
    t, d = x2.shape
    n_tiles = t // ROW_TILE
    dest = dest.reshape(n_tiles, 1, TOP_K * ROW_TILE)
    dest_block = lambda index_map: pl.BlockSpec((1, 1, TOP_K * ROW_TILE), index_map, memory_space=pltpu.SMEM)
    return pl.pallas_call(
        _combine_ln_body,
        grid=(n_tiles,),
        in_specs=[dest_block(lambda i: (i, 0, 0)),
                  dest_block(lambda i: (jnp.minimum(i + 1, n_tiles - 1), 0, 0)),
                  pl.BlockSpec((ROW_TILE, d), lambda i: (i, 0)),
                  pl.BlockSpec((ROW_TILE, LANES), lambda i: (i, 0)),
                  pl.BlockSpec(memory_space=pl.ANY),
                  _whole(g), _whole(b)],
        out_specs=pl.BlockSpec((ROW_TILE, d), lambda i: (i, 0)),
        out_shape=jax.ShapeDtypeStruct((t, d), F32),
        scratch_shapes=[pltpu.VMEM((2, TOP_K, ROW_TILE * SUBLANES, LANES), F32), pltpu.SemaphoreType.DMA((2,))],
        compiler_params=_params("arbitrary"),
        name="combine_ln",
    )(dest, dest, x2, meta, yb, g, b)


def _moe_ln(layer, x2, w_router, wg, wu, wd, g, b):
    t, _ = x2.shape
    meta, counts = _router(x2, w_router)
    counts = counts[0, :N_EXPERTS].astype(jnp.int32)
    padded = (counts + EXPERT_BLOCK - 1) // EXPERT_BLOCK * EXPERT_BLOCK
    pad_end = jnp.cumsum(padded)
    pad_start = pad_end - padded
    experts = meta[:, _COL_E0:_COL_E1 + 1].astype(jnp.int32)
    ranks = meta[:, _COL_R0:_COL_R1 + 1].astype(jnp.int32)
    dest = ranks
    for e in range(N_EXPERTS):
        dest = dest + jnp.where(experts == e, pad_start[e], 0)
    n_blocks = t * TOP_K // EXPERT_BLOCK + N_EXPERTS
    block_start = jnp.arange(n_blocks, dtype=jnp.int32) * EXPERT_BLOCK
    block_e = jnp.minimum(jnp.sum(pad_end[None, :] <= block_start[:, None], axis=1), N_EXPERTS - 1).astype(jnp.int32)
    n_used = (pad_end[-1:] // EXPERT_BLOCK).astype(jnp.int32)
    n_rows = n_blocks * EXPERT_BLOCK
    xg = _dispatch(x2, dest, pad_end.astype(jnp.int32), n_rows)
    yb = _experts(layer, xg.reshape(n_rows * SUBLANES, LANES), block_e, n_used, wg, wu, wd)
    return _combine_ln(x2, meta, yb.reshape(n_rows, SUBLANES, LANES), dest, g, b)


def _block_diag(w):
    n_g, c, _ = w.shape
    out = jnp.zeros((n_g * c, n_g * c), w.dtype)
    for gi in range(n_g):
        out = out.at[gi * c:(gi + 1) * c, gi * c:(gi + 1) * c].set(w[gi])
    return out


def kernel(x, mem, w_mem_kv, a_w_in, a_conv_w, a_w_out, b_w_in, b_w_out, c_w_in, c_pool_w, c_pool_scale, c_w_out, ln_g, ln_b, ffn_w_gate, ffn_w_up, ffn_w_down, moe_router, moe_w_gate, moe_w_up, moe_w_down):
    bsz, seq_len, d = x.shape
    t = bsz * seq_len
    kt, v = _mem_kv(mem, w_mem_kv.astype(BF16))
    x = x.astype(F32)
    a_w_in, a_w_out, b_w_in, b_w_out, c_w_in, c_w_out = (
        w.astype(BF16) for w in (a_w_in, a_w_out, b_w_in, b_w_out, c_w_in, c_w_out))
    ffn_w = [w.astype(BF16) for w in (ffn_w_gate, ffn_w_up, ffn_w_down)]
    moe_w = [w.astype(BF16) for w in (moe_w_gate, moe_w_up, moe_w_down)]
    for i in range(DEPTH):
        kind, j = i % 3, i // 3
        g0, b0 = ln_g[i, 0].reshape(1, d), ln_b[i, 0].reshape(1, d)
        g1, b1 = ln_g[i, 1].reshape(1, d), ln_b[i, 1].reshape(1, d)
        if kind == 0:
            x = _mixer_layer(_conv_layer_body, j, x, a_w_in, [a_conv_w[j]], kt, v, a_w_out, g0, b0)
        elif kind == 1:
            h = _proj_in(j, x.reshape(t, d), b_w_in).reshape(bsz, seq_len, -1)
            a = _attention(h, kt, v)
            x = _proj_out_ln(j, a.reshape(t, -1), x.reshape(t, d), b_w_out, g0, b0)
            x = x.reshape(bsz, seq_len, d)
        else:
            x = _mixer_layer(_pool_layer_body, j, x, c_w_in,
                             [_block_diag(c_pool_w[j]).astype(BF16), c_pool_scale[j].reshape(1, -1)], kt, v,
                             c_w_out, g0, b0)
        f = i // 2
        x2 = x.reshape(t, d)
        if i % 2 == 0:
            x2 = _ffn_ln(f, x2, *ffn_w, g1, b1)
        else:
            x2 = _moe_ln(f, x2, moe_router[f], *moe_w, g1, b1)
        x = x2.reshape(bsz, seq_len, d)
    return x
```

```python
import functools

import jax
import jax.numpy as jnp
from jax import lax
from jax.experimental import pallas as pl
from jax.experimental.pallas import tpu as pltpu

F32 = jnp.float32
BF16 = jnp.bfloat16

D_MODEL = 1024
DEPTH = 4
HEAD_DIM = 64
MEM_HEADS = 4
MEM_WIDTH = MEM_HEADS * HEAD_DIM
MIX_WIDTH = 3 * D_MODEL // 4
DIL_GROUPS = ((128, 1), (512, 4), (2048, 16))
DIL_HEADS = 4
DIL_WIDTH = DIL_HEADS * HEAD_DIM
DIL_RADIUS = 64
POOL_WINDOWS = (2, 4, 8, 16)
POOL_GROUP = MIX_WIDTH // len(POOL_WINDOWS)
N_EXPERTS = 8
TOP_K = 2
LN_EPS = 1e-5
NEG_INF = -1e30
ALPHA = (2 * DEPTH) ** 0.25

LANES = 128
SUBLANES = 8
HALO = SUBLANES
SEQ_TILE = 1024
SUB_TILE = 512
ROW_TILE = 512
MATMUL_TILE = 1024
F_CHUNK = 768
DISPATCH_TILE = 1024
EXPERT_BLOCK = 512
EXPERT_F_TILES = 2
Q_BLOCK = 128
ISSUE_UNROLL = 8
VMEM_LIMIT = 56 * 1024 * 1024


def _params(*sem):
    return pltpu.CompilerParams(dimension_semantics=sem, vmem_limit_bytes=VMEM_LIMIT)


def _layer_norm(v, g, b):
    mu = jnp.mean(v, axis=-1, keepdims=True)
    d = v - mu
    var = jnp.mean(d * d, axis=-1, keepdims=True)
    return d * lax.rsqrt(var + LN_EPS) * g + b


def _dot(a, b):
    return jnp.dot(a, b, preferred_element_type=F32)


def _head_masks(width):
    col = lax.broadcasted_iota(jnp.int32, (1, width), 1)
    return [(col >= h * HEAD_DIM) & (col < (h + 1) * HEAD_DIM) for h in range(width // HEAD_DIM)]


def _stack_heads(q, masks):
    return jnp.concatenate([jnp.where(m, q, 0.0) for m in masks], axis=0).astype(BF16)


def _unstack_heads(o, masks, n):
    out = o[(len(masks) - 1) * n:]
    for h in range(len(masks) - 2, -1, -1):
        out = jnp.where(masks[h], o[h * n:(h + 1) * n], out)
    return out


def _mem_attention(q, kt, v):
    n = q.shape[0]
    masks = _head_masks(MEM_WIDTH)
    sc = _dot(_stack_heads(q, masks), kt) * HEAD_DIM ** -0.5
    p = jnp.exp(sc - jnp.max(sc, axis=-1, keepdims=True))
    p = p / jnp.sum(p, axis=-1, keepdims=True)
    return _unstack_heads(_dot(p.astype(BF16), v), masks, n)


def _mem_kv_body(mem_ref, w_ref, kt_ref, v_ref):
    kv = _dot(mem_ref[...].astype(BF16), w_ref[...])
    kt_ref[...] = kv[:, :MEM_WIDTH].T.astype(BF16)
    v_ref[...] = kv[:, MEM_WIDTH:].astype(BF16)


def _mem_kv(mem, w_kv):
    b, m, d = mem.shape
    return pl.pallas_call(
        _mem_kv_body,
        grid=(b,),
        in_specs=[pl.BlockSpec((None, m, d), lambda i: (i, 0, 0)),
                  pl.BlockSpec((d, 2 * MEM_WIDTH), lambda i: (0, 0))],
        out_specs=[pl.BlockSpec((None, MEM_WIDTH, m), lambda i: (i, 0, 0)),
                   pl.BlockSpec((None, m, MEM_WIDTH), lambda i: (i, 0, 0))],
        out_shape=[jax.ShapeDtypeStruct((b, MEM_WIDTH, m), BF16),
                   jax.ShapeDtypeStruct((b, m, MEM_WIDTH), BF16)],
        compiler_params=_params("arbitrary"),
        name="mem_kv",
    )(mem, w_kv)


def _sub_tiles(xp_ref, xm_ref, xn_ref, seq_len):
    ts = xm_ref.shape[0]
    first = pl.program_id(1) * ts
    xm = xm_ref[...]
    xe = jnp.concatenate([xp_ref[...], xm, xn_ref[...]], axis=0).astype(BF16)
    pos = first - HALO + lax.broadcasted_iota(jnp.int32, (ts + 2 * HALO, 1), 0)
    valid = (pos >= 0) & (pos < seq_len)
    for r0 in range(0, ts, SUB_TILE):
        ext = slice(r0, r0 + SUB_TILE + 2 * HALO)
        yield r0, first + r0, xm[r0:r0 + SUB_TILE], xe[ext], valid[ext]


def _shift_rows(a, k):
    n = a.shape[0]
    return pltpu.roll(a, k % n, 0)


def _mixer_tail(mix, q_mem, xm, kt_ref, v_ref, wout_ref, g_ref, b_ref, o_ref, r0):
    mem_out = _mem_attention(q_mem, kt_ref[...], v_ref[...])
    y = (_dot(mix.astype(BF16), wout_ref[:MIX_WIDTH, :])
         + _dot(mem_out.astype(BF16), wout_ref[MIX_WIDTH:, :]))
    o_ref[r0:r0 + SUB_TILE, :] = _layer_norm(ALPHA * xm + y, g_ref[...], b_ref[...])


def _conv_layer_body(seq_len, xp_ref, xm_ref, xn_ref, win_ref, cw_ref, kt_ref, v_ref,
                     wout_ref, g_ref, b_ref, o_ref):
    n = SUB_TILE
    cw = cw_ref[...]
    for r0, _, xm, xe, valid in _sub_tiles(xp_ref, xm_ref, xn_ref, seq_len):
        xmb = xm.astype(BF16)
        gate_b = _dot(xmb, win_ref[:, :MIX_WIDTH])
        cu = _dot(xe, win_ref[:, MIX_WIDTH:3 * MIX_WIDTH])
        q_mem = _dot(xmb, win_ref[:, 3 * MIX_WIDTH:])
        z = jnp.where(valid, cu[:, :MIX_WIDTH] * cu[:, MIX_WIDTH:], 0.0)
        conv = (cw[0:1] * _shift_rows(z, 1)[HALO:HALO + n]
                + cw[1:2] * z[HALO:HALO + n]
                + cw[2:3] * _shift_rows(z, -1)[HALO:HALO + n])
        _mixer_tail(gate_b * conv, q_mem, xm, kt_ref, v_ref, wout_ref, g_ref, b_ref, o_ref, r0)


def _pool_layer_body(seq_len, xp_ref, xm_ref, xn_ref, win_ref, pw_ref, ps_ref, kt_ref, v_ref,
                     wout_ref, g_ref, b_ref, o_ref):
    n = SUB_TILE
    col = lax.broadcasted_iota(jnp.int32, (1, MIX_WIDTH), 1)
    for r0, first, xm, xe, valid in _sub_tiles(xp_ref, xm_ref, xn_ref, seq_len):
        u = jnp.where(valid, _dot(xe, win_ref[:, :MIX_WIDTH]), 0.0)
        q_mem = _dot(xm.astype(BF16), win_ref[:, MIX_WIDTH:])
        a2 = u + _shift_rows(u, 1)
        a4 = _shift_rows(a2, 1) + _shift_rows(a2, -1)
        a8 = _shift_rows(a4, 2) + _shift_rows(a4, -2)
        a16 = _shift_rows(a8, 4) + _shift_rows(a8, -4)
        pos = first + lax.broadcasted_iota(jnp.int32, (n, 1), 0)
        num = a16[HALO:HALO + n]
        cnt = None
        for gi in range(len(POOL_WINDOWS) - 1, -1, -1):
            w = POOL_WINDOWS[gi]
            c_w = (jnp.minimum(pos + (w // 2 - 1), seq_len - 1) - jnp.maximum(pos - w // 2, 0) + 1).astype(F32)
            if cnt is None:
                cnt = jnp.broadcast_to(c_w, (n, MIX_WIDTH))
            else:
                in_group = col < (gi + 1) * POOL_GROUP
                num = jnp.where(in_group, (a2, a4, a8)[gi][HALO:HALO + n], num)
                cnt = jnp.where(in_group, c_w, cnt)
        diff = num / cnt - u[HALO:HALO + n]
        mix = _dot(diff.astype(BF16), pw_ref[...]) * ps_ref[...]
        _mixer_tail(mix, q_mem, xm, kt_ref, v_ref, wout_ref, g_ref, b_ref, o_ref, r0)


def _whole(w, layer=None):
    if layer is None:
        return pl.BlockSpec(w.shape, lambda *_: (0,) * w.ndim)
    return pl.BlockSpec((None,) + w.shape[1:], lambda *_: (layer,) + (0,) * (w.ndim - 1))


def _mixer_layer(body, layer, x, w_in, extra, kt, v, w_out, g, b):
    bsz, seq_len, d = x.shape
    ts = SEQ_TILE
    n_halo_blocks = seq_len // HALO
    in_specs = [
        pl.BlockSpec((None, HALO, d), lambda i, s: (i, jnp.maximum(s * (ts // HALO) - 1, 0), 0)),
        pl.BlockSpec((None, ts, d), lambda i, s: (i, s, 0)),
        pl.BlockSpec((None, HALO, d), lambda i, s: (i, jnp.minimum((s + 1) * (ts // HALO), n_halo_blocks - 1), 0)),
        _whole(w_in, layer),
    ]
    in_specs += [_whole(e) for e in extra]
    in_specs += [
        pl.BlockSpec((None,) + kt.shape[1:], lambda i, s: (i, 0, 0)),
        pl.BlockSpec((None,) + v.shape[1:], lambda i, s: (i, 0, 0)),
        _whole(w_out, layer),
        _whole(g),
        _whole(b),
    ]
    return pl.pallas_call(
        functools.partial(body, seq_len),
        grid=(bsz, seq_len // ts),
        in_specs=in_specs,
        out_specs=pl.BlockSpec((None, ts, d), lambda i, s: (i, s, 0)),
        out_shape=jax.ShapeDtypeStruct(x.shape, F32),
        compiler_params=_params("arbitrary", "arbitrary"),
        name=body.__name__.strip("_"),
    )(x, x, x, w_in, *extra, kt, v, w_out, g, b)


def _proj_in_body(x_ref, w_ref, o_ref):
    o_ref[...] = _dot(x_ref[...].astype(BF16), w_ref[...])


def _proj_in(layer, x2, w):
    t, d = x2.shape
    n = w.shape[-1]
    return pl.pallas_call(
        _proj_in_body,
        grid=(t // MATMUL_TILE,),
        in_specs=[pl.BlockSpec((MATMUL_TILE, d), lambda i: (i, 0)),
                  _whole(w, layer)],
        out_specs=pl.BlockSpec((MATMUL_TILE, n), lambda i: (i, 0)),
        out_shape=jax.ShapeDtypeStruct((t, n), F32),
        compiler_params=_params("arbitrary"),
        name="proj_in",
    )(x2, w)


def _alibi_slope(index, total):
    return 2.0 ** (-8.0 * (index + 1) / total)


def _rows(start, size, stride):
    return pl.ds(start, size) if stride == 1 else pl.ds(start, size, stride=stride)


def _load_cols(refs, rows):
    return jnp.concatenate([r[rows, :] for r in refs], axis=1)


def _store_cols(refs, rows, val):
    for c, r in enumerate(refs):
        r[rows, :] = val[:, c * LANES:(c + 1) * LANES]


def _dilated_group(gi, dilation, seq_len, hq_refs, hk_refs, hv_refs, m_refs, l_refs, o_refs):
    n_sub = seq_len // dilation
    qb = min(Q_BLOCK, n_sub)
    kw = min(qb + 2 * DIL_RADIUS, n_sub)
    masks = _head_masks(DIL_WIDTH)
    n_heads_total = len(DIL_GROUPS) * DIL_HEADS
    bias_cache = {}

    def bias_and_mask(offset):
        if offset not in bias_cache:
            rel = offset + lax.broadcasted_iota(jnp.int32, (qb, kw), 0) - lax.broadcasted_iota(jnp.int32, (qb, kw), 1)
            dist = jnp.abs(rel)
            inside = dist <= DIL_RADIUS
            span = (dist * dilation).astype(F32)
            bias = jnp.concatenate(
                [-_alibi_slope(gi * DIL_HEADS + h, n_heads_total) * span for h in range(DIL_HEADS)], axis=0)
            bias_cache[offset] = (bias, jnp.concatenate([inside] * DIL_HEADS, axis=0))
        return bias_cache[offset]

    for r in range(dilation):
        for j0 in range(0, n_sub, qb):
            ks = min(max(j0 - DIL_RADIUS, 0), n_sub - kw)
            q_rows = _rows(r + j0 * dilation, qb, dilation)
            k_rows = _rows(r + ks * dilation, kw, dilation)
            q = _load_cols(hq_refs, q_rows)
            k = _load_cols(hk_refs, k_rows).astype(BF16)
            v = _load_cols(hv_refs, k_rows).astype(BF16)
            sc = lax.dot_general(_stack_heads(q, masks), k, (((1,), (1,)), ((), ())),
                                 preferred_element_type=F32) * HEAD_DIM ** -0.5
            bias, inside = bias_and_mask(j0 - ks)
            sc = jnp.where(inside, sc + bias, NEG_INF)
            mx = jnp.max(sc, axis=-1, keepdims=True)
            p = jnp.exp(sc - mx)
            den = jnp.sum(p, axis=-1, keepdims=True)
            num = _unstack_heads(_dot(p.astype(BF16), v), masks, qb)
            mx = _unstack_heads(jnp.broadcast_to(mx, (DIL_HEADS * qb, DIL_WIDTH)), masks, qb)
            den = _unstack_heads(jnp.broadcast_to(den, (DIL_HEADS * qb, DIL_WIDTH)), masks, qb)
            if gi > 0:
                m_old = _load_cols(m_refs, q_rows)
                m_new = jnp.maximum(m_old, mx)
                a_old = jnp.exp(m_old - m_new)
                a_new = jnp.exp(mx - m_new)
                mx = m_new
                den = _load_cols(l_refs, q_rows) * a_old + den * a_new
                num = _load_cols(o_refs, q_rows) * a_old + num * a_new
            _store_cols(m_refs, q_rows, mx)
            _store_cols(l_refs, q_rows, den)
            _store_cols(o_refs, q_rows, num)


def _attention_body(*refs):
    n_slab = DIL_WIDTH // LANES
    hq_refs, hk_refs, hv_refs = (refs[i * n_slab:(i + 1) * n_slab] for i in range(3))
    hm_ref, kt_ref, v_ref, out_ref = refs[3 * n_slab:3 * n_slab + 4]
    m_refs, l_refs, o_refs = (refs[3 * n_slab + 4 + i * n_slab:3 * n_slab + 4 + (i + 1) * n_slab] for i in range(3))
    g = pl.program_id(1)
    seq_len = hm_ref.shape[0]

    @pl.when(g == 0)
    def _():
        for c in range(0, seq_len, ROW_TILE):
            mem_out = _mem_attention(hm_ref[c:c + ROW_TILE, :], kt_ref[...], v_ref[...])
            out_ref[c:c + ROW_TILE, DIL_WIDTH:] = mem_out.astype(BF16)

    for gi, (_, dilation) in enumerate(DIL_GROUPS):
        @pl.when(g == gi)
        def _(gi=gi, dilation=dilation):
            _dilated_group(gi, dilation, seq_len, hq_refs, hk_refs, hv_refs, m_refs, l_refs, o_refs)

    @pl.when(g == len(DIL_GROUPS) - 1)
    def _():
        for c in range(n_slab):
            out_ref[:, c * LANES:(c + 1) * LANES] = (o_refs[c][...] / l_refs[c][...]).astype(BF16)


def _attention(h, kt, v):
    bsz, seq_len, _ = h.shape
    n_g = len(DIL_GROUPS)
    n_slab = DIL_WIDTH // LANES

    def slabs(base):
        return [pl.BlockSpec((None, seq_len, LANES), lambda i, g, c=c: (i, 0, (base + g) * n_slab + c))
                for c in range(n_slab)]

    return pl.pallas_call(
        _attention_body,
        grid=(bsz, n_g),
        in_specs=slabs(0) + slabs(n_g) + slabs(2 * n_g) + [
            pl.BlockSpec((None, seq_len, MEM_WIDTH), lambda i, g: (i, 0, 3 * n_g)),
            pl.BlockSpec((None,) + kt.shape[1:], lambda i, g: (i, 0, 0)),
            pl.BlockSpec((None,) + v.shape[1:], lambda i, g: (i, 0, 0))],
        out_specs=pl.BlockSpec((None, seq_len, DIL_WIDTH + MEM_WIDTH), lambda i, g: (i, 0, 0)),
        out_shape=jax.ShapeDtypeStruct((bsz, seq_len, DIL_WIDTH + MEM_WIDTH), BF16),
        scratch_shapes=[pltpu.VMEM((seq_len, LANES), F32)] * (3 * n_slab),
        compiler_params=_params("arbitrary", "arbitrary"),
        name="dilated_attention",
    )(*([h] * (3 * n_slab + 1)), kt, v)


def _proj_out_ln_body(a_ref, x_ref, w_ref, g_ref, b_ref, o_ref):
    y = _dot(a_ref[...], w_ref[...])
    o_ref[...] = _layer_norm(ALPHA * x_ref[...] + y, g_ref[...], b_ref[...])


def _proj_out_ln(layer, a2, x2, w, g, b):
    t, d = x2.shape
    k = a2.shape[1]
    return pl.pallas_call(
        _proj_out_ln_body,
        grid=(t // ROW_TILE,),
        in_specs=[pl.BlockSpec((ROW_TILE, k), lambda i: (i, 0)),
                  pl.BlockSpec((ROW_TILE, d), lambda i: (i, 0)),
                  _whole(w, layer), _whole(g), _whole(b)],
        out_specs=pl.BlockSpec((ROW_TILE, d), lambda i: (i, 0)),
        out_shape=jax.ShapeDtypeStruct((t, d), F32),
        compiler_params=_params("arbitrary"),
        name="proj_out_ln",
    )(a2, x2, w, g, b)


def _swiglu(xb, wg_ref, wu_ref, wd_ref, f_chunk):
    y = None
    f = wg_ref.shape[1]
    for c in range(0, f, f_chunk):
        e = min(c + f_chunk, f)
        g = _dot(xb, wg_ref[:, c:e])
        u = _dot(xb, wu_ref[:, c:e])
        part = _dot((g * jax.nn.sigmoid(g) * u).astype(BF16), wd_ref[c:e, :])
        y = part if y is None else y + part
    return y


def _ffn_ln_body(x_ref, wg_ref, wu_ref, wd_ref, g_ref, b_ref, o_ref):
    x = x_ref[...]
    y = _swiglu(x.astype(BF16), wg_ref, wu_ref, wd_ref, F_CHUNK)
    o_ref[...] = _layer_norm(ALPHA * x + y, g_ref[...], b_ref[...])


def _ffn_ln(layer, x2, wg, wu, wd, g, b):
    t, d = x2.shape
    f = wg.shape[-1]
    resident = lambda w: pl.BlockSpec((None,) + w.shape[1:], lambda i: (layer, 0, 0), pipeline_mode=pl.Buffered(1))
    return pl.pallas_call(
        _ffn_ln_body,
        grid=(t // MATMUL_TILE,),
        in_specs=[pl.BlockSpec((MATMUL_TILE, d), lambda i: (i, 0)),
                  resident(wg), resident(wu), resident(wd), _whole(g), _whole(b)],
        out_specs=pl.BlockSpec((MATMUL_TILE, d), lambda i: (i, 0)),
        out_shape=jax.ShapeDtypeStruct((t, d), F32),
        compiler_params=_params("arbitrary"),
        name="ffn_ln",
    )(x2, wg, wu, wd, g, b)


_COL_E0, _COL_E1, _COL_R0, _COL_R1, _COL_W0, _COL_W1 = range(6)


def _split_bf16(a):
    hi = a.astype(BF16)
    return hi, (a - hi.astype(F32)).astype(BF16)


def _router_body(x_ref, whl_ref, tri_ref, meta_ref, route_ref, cnt_ref, carry_ref):
    i = pl.program_id(0)

    @pl.when(i == 0)
    def _():
        carry_ref[...] = jnp.zeros_like(carry_ref)

    tm = x_ref.shape[0]
    lane = lax.broadcasted_iota(jnp.int32, (tm, LANES), 1).astype(F32)
    xh, xl = _split_bf16(x_ref[...])
    both = _dot(xh, whl_ref[...])
    logits = both[:, :LANES] + (_dot(xl, whl_ref[:, :LANES]) + both[:, LANES:])
    logits = jnp.where(lane < N_EXPERTS, logits, -jnp.inf)
    m0 = jnp.max(logits, axis=-1, keepdims=True)
    e0 = jnp.min(jnp.where(logits == m0, lane, float(LANES)), axis=-1, keepdims=True)
    rest = jnp.where(lane == e0, -jnp.inf, logits)
    m1 = jnp.max(rest, axis=-1, keepdims=True)
    e1 = jnp.min(jnp.where(rest == m1, lane, float(LANES)), axis=-1, keepdims=True)
    ex = jnp.exp(m1 - m0)
    w0 = 1.0 / (1.0 + ex)
    w1 = ex / (1.0 + ex)
    hit0 = lane == e0
    hit1 = lane == e1
    onehot = (hit0 | hit1).astype(F32)
    before = _dot(tri_ref[...], onehot.astype(BF16)) + carry_ref[...]
    r0 = jnp.sum(jnp.where(hit0, before, 0.0), axis=-1, keepdims=True)
    r1 = jnp.sum(jnp.where(hit1, before, 0.0), axis=-1, keepdims=True)
    carry_ref[...] += jnp.sum(onehot, axis=0, keepdims=True)
    cnt_ref[...] = carry_ref[...]
    meta = jnp.zeros((tm, LANES), F32)
    for col, val in ((_COL_E0, e0), (_COL_E1, e1), (_COL_R0, r0), (_COL_R1, r1), (_COL_W0, w0), (_COL_W1, w1)):
        meta = jnp.where(lane == col, val, meta)
    meta_ref[...] = meta
    route_ref[...] = meta.T[:SUBLANES, :]


def _router(x2, w_router):
    t, d = x2.shape
    whl = jnp.concatenate(_split_bf16(jnp.zeros((d, LANES), F32).at[:, :N_EXPERTS].set(w_router)), axis=1)
    tri = jnp.tril(jnp.ones((ROW_TILE, ROW_TILE), BF16), -1)
    return pl.pallas_call(
        _router_body,
        grid=(t // ROW_TILE,),
        in_specs=[pl.BlockSpec((ROW_TILE, d), lambda i: (i, 0)), _whole(whl), _whole(tri)],
        out_specs=[pl.BlockSpec((ROW_TILE, LANES), lambda i: (i, 0)),
                   pl.BlockSpec((None, SUBLANES, ROW_TILE), lambda i: (i, 0, 0)),
                   pl.BlockSpec((1, LANES), lambda i: (0, 0))],
        out_shape=[jax.ShapeDtypeStruct((t, LANES), F32),
                   jax.ShapeDtypeStruct((t // ROW_TILE, SUBLANES, ROW_TILE), F32),
                   jax.ShapeDtypeStruct((1, LANES), F32)],
        scratch_shapes=[pltpu.VMEM((1, LANES), F32)],
        compiler_params=_params("arbitrary"),
        name="router",
    )(x2, whl, tri)


assert D_MODEL == SUBLANES * LANES


def _store_row_tiles(ref, val):
    n = val.shape[0]
    for c in range(SUBLANES):
        ref[pl.ds(c, n, stride=SUBLANES), :] = val[:, c * LANES:(c + 1) * LANES]


def _load_row_tiles(ref, n):
    return jnp.concatenate([ref[pl.ds(c, n, stride=SUBLANES), :] for c in range(SUBLANES)], axis=1)


def _tile_of(ref, row):
    return ref.at[pl.ds(pl.multiple_of(row * SUBLANES, SUBLANES), SUBLANES), :]


def _wait_rows(hbm_rows_ref, n, sem):
    span = hbm_rows_ref.at[pl.ds(0, n)]
    pltpu.make_async_copy(span, span, sem).wait()


def _issue_row_copies(n_tokens, copy_of):
    def group(gidx, _):
        for u in range(ISSUE_UNROLL):
            for k in range(TOP_K):
                copy_of(gidx * ISSUE_UNROLL + u, k).start(priority=k)
        return 0

    lax.fori_loop(0, n_tokens // ISSUE_UNROLL, group, 0)


def _dispatch_body(pad_end_ref, dest_ref, x_ref, xg_ref, stage_ref, zero_ref, sem, zero_sem):
    tm = x_ref.shape[0]

    @pl.when(pl.program_id(0) == 0)
    def _():
        zero_ref[...] = jnp.zeros_like(zero_ref)

        def fill(start):
            copy = pltpu.make_async_copy(zero_ref, xg_ref.at[pl.ds(start, EXPERT_BLOCK)], zero_sem)
            copy.start()
            copy.wait()

        for e in range(N_EXPERTS):
            fill(jnp.maximum(pad_end_ref[e] - EXPERT_BLOCK, 0))
            tail = pad_end_ref[N_EXPERTS - 1] + e * EXPERT_BLOCK
            pl.when(tail < xg_ref.shape[0])(functools.partial(fill, tail))

    _store_row_tiles(stage_ref, x_ref[...])
    for part in range(tm // ROW_TILE):
        _issue_row_copies(ROW_TILE, lambda tok, k, part=part: pltpu.make_async_copy(
            _tile_of(stage_ref, part * ROW_TILE + tok), xg_ref.at[dest_ref[part, k, tok]], sem))
    for _ in range(TOP_K):
        _wait_rows(xg_ref, tm, sem)


def _dispatch(x2, dest, pad_end, n_rows):
    t, d = x2.shape
    tm = DISPATCH_TILE
    n_tiles = t // tm
    grid_spec = pltpu.PrefetchScalarGridSpec(
        num_scalar_prefetch=1,
        grid=(n_tiles,),
        in_specs=[pl.BlockSpec((tm // ROW_TILE, TOP_K, ROW_TILE), lambda i, pe: (i, 0, 0), memory_space=pltpu.SMEM),
                  pl.BlockSpec((tm, d), lambda i, pe: (i, 0))],
        out_specs=pl.BlockSpec(memory_space=pl.ANY),
        scratch_shapes=[pltpu.VMEM((tm * SUBLANES, LANES), F32),
                        pltpu.VMEM((EXPERT_BLOCK, SUBLANES, LANES), F32),
                        pltpu.SemaphoreType.DMA(()), pltpu.SemaphoreType.DMA(())],
    )
    return pl.pallas_call(
        _dispatch_body,
        grid_spec=grid_spec,
        out_shape=jax.ShapeDtypeStruct((n_rows, SUBLANES, LANES), F32),
        compiler_params=_params("arbitrary"),
        name="dispatch",
    )(pad_end, dest, x2)


def _experts_body(be_ref, nused_ref, x_ref, wg_ref, wu_ref, wd_ref, o_ref, acc_ref):
    blk = pl.program_id(0)
    j = pl.program_id(1)

    @pl.when(blk < nused_ref[0])
    def _():
        xb = _load_row_tiles(x_ref, EXPERT_BLOCK).astype(BF16)
        y = _swiglu(xb, wg_ref, wu_ref, wd_ref, F_CHUNK)

        @pl.when(j == 0)
        def _():
            acc_ref[...] = y

        @pl.when(j == EXPERT_F_TILES - 1)
        def _():
            _store_row_tiles(o_ref, acc_ref[...] + y)

    @pl.when(blk >= nused_ref[0])
    def _():
        o_ref[...] = jnp.zeros_like(o_ref)


def _experts(layer, xg, block_e, n_used, wg, wu, wd):
    assert EXPERT_F_TILES == 2
    n_rows = xg.shape[0] // SUBLANES
    d = D_MODEL
    f = wg.shape[-1]
    ft = f // EXPERT_F_TILES
    n_blocks = n_rows // EXPERT_BLOCK
    last = EXPERT_F_TILES - 1
    tile_rows = EXPERT_BLOCK * SUBLANES

    def used(blk, nu):
        return jnp.maximum(jnp.minimum(blk, nu[0] - 1), 0)

    def f_tile(blk, j, nu):
        return jnp.where(blk < nu[0], j, last)

    grid_spec = pltpu.PrefetchScalarGridSpec(
        num_scalar_prefetch=2,
        grid=(n_blocks, EXPERT_F_TILES),
        in_specs=[pl.BlockSpec((tile_rows, LANES), lambda blk, j, be, nu: (used(blk, nu), 0)),
                  pl.BlockSpec((None, None, d, ft),
                               lambda blk, j, be, nu: (layer, be[used(blk, nu)], 0, f_tile(blk, j, nu))),
                  pl.BlockSpec((None, None, d, ft),
                               lambda blk, j, be, nu: (layer, be[used(blk, nu)], 0, f_tile(blk, j, nu))),
                  pl.BlockSpec((None, None, ft, d),
                               lambda blk, j, be, nu: (layer, be[used(blk, nu)], f_tile(blk, j, nu), 0))],
        out_specs=pl.BlockSpec((tile_rows, LANES), lambda blk, j, be, nu: (blk, 0)),
        scratch_shapes=[pltpu.VMEM((EXPERT_BLOCK, d), F32)],
    )
    return pl.pallas_call(
        _experts_body,
        grid_spec=grid_spec,
        out_shape=jax.ShapeDtypeStruct(xg.shape, F32),
        compiler_params=_params("arbitrary", "arbitrary"),
        name="experts",
    )(block_e, n_used, xg, wg, wu, wd)


def _combine_ln_body(dest_ref, dest_next_ref, x_ref, w_ref, yb_ref, g_ref, b_ref, o_ref, buf_ref, sem):
    i = pl.program_id(0)
    tm = x_ref.shape[0]
    slot = lax.rem(i, 2)

    def gather(idx_ref, into):
        _issue_row_copies(tm, lambda tok, k: pltpu.make_async_copy(
            yb_ref.at[idx_ref[0, k, tok]], _tile_of(buf_ref.at[into, k], tok), sem.at[into]))

    @pl.when(i == 0)
    def _():
        gather(dest_ref, 0)

    @pl.when(i + 1 < pl.num_programs(0))
    def _():
        gather(dest_next_ref, 1 - slot)

    for _ in range(TOP_K):
        _wait_rows(yb_ref, tm, sem.at[slot])
    w = w_ref[...]
    y = (w[:, _COL_W0:_COL_W0 + 1] * _load_row_tiles(buf_ref.at[slot, 0], tm)
         + w[:, _COL_W1:_COL_W1 + 1] * _load_row_tiles(buf_ref.at[slot, 1], tm))
    o_ref[...] = _layer_norm(ALPHA * x_ref[...] + y, g_ref[...], b_ref[...])


def _combine_ln(x2, meta, yb, dest, g, b):
    t, d = x2.shape
    n_tiles = t // ROW_TILE
    dest_block = lambda index_map: pl.BlockSpec((1, TOP_K, ROW_TILE), index_map, memory_space=pltpu.SMEM)
    return pl.pallas_call(
        _combine_ln_body,
        grid=(n_tiles,),
        in_specs=[dest_block(lambda i: (i, 0, 0)),
                  dest_block(lambda i: (jnp.minimum(i + 1, n_tiles - 1), 0, 0)),
                  pl.BlockSpec((ROW_TILE, d), lambda i: (i, 0)),
                  pl.BlockSpec((ROW_TILE, LANES), lambda i: (i, 0)),
                  pl.BlockSpec(memory_space=pl.ANY),
                  _whole(g), _whole(b)],
        out_specs=pl.BlockSpec((ROW_TILE, d), lambda i: (i, 0)),
        out_shape=jax.ShapeDtypeStruct((t, d), F32),
        scratch_shapes=[pltpu.VMEM((2, TOP_K, ROW_TILE * SUBLANES, LANES), F32), pltpu.SemaphoreType.DMA((2,))],
        compiler_params=_params("arbitrary"),
        name="combine_ln",
    )(dest, dest, x2, meta, yb, g, b)


def _moe_ln(layer, x2, w_router, wg, wu, wd, g, b):
    t, _ = x2.shape
    meta, routes, counts = _router(x2, w_router)
    counts = counts[0, :N_EXPERTS].astype(jnp.int32)
    padded = (counts + EXPERT_BLOCK - 1) // EXPERT_BLOCK * EXPERT_BLOCK
    pad_end = jnp.cumsum(padded)
    pad_start = pad_end - padded
    experts = routes[:, _COL_E0:_COL_E1 + 1, :].astype(jnp.int32)
    ranks = routes[:, _COL_R0:_COL_R1 + 1, :].astype(jnp.int32)
    dest = ranks
    for e in range(N_EXPERTS):
        dest = dest + jnp.where(experts == e, pad_start[e], 0)
    n_blocks = t * TOP_K // EXPERT_BLOCK + N_EXPERTS
    block_start = jnp.arange(n_blocks, dtype=jnp.int32) * EXPERT_BLOCK
    block_e = jnp.minimum(jnp.sum(pad_end[None, :] <= block_start[:, None], axis=1), N_EXPERTS - 1).astype(jnp.int32)
    n_used = (pad_end[-1:] // EXPERT_BLOCK).astype(jnp.int32)
    n_rows = n_blocks * EXPERT_BLOCK
    xg = _dispatch(x2, dest, pad_end.astype(jnp.int32), n_rows)
    yb = _experts(layer, xg.reshape(n_rows * SUBLANES, LANES), block_e, n_used, wg, wu, wd)
    return _combine_ln(x2, meta, yb.reshape(n_rows, SUBLANES, LANES), dest, g, b)


def _block_diag(w):
    n_g, c, _ = w.shape
    out = jnp.zeros((n_g * c, n_g * c), w.dtype)
    for gi in range(n_g):
        out = out.at[gi * c:(gi + 1) * c, gi * c:(gi + 1) * c].set(w[gi])
    return out


def kernel(x, mem, w_mem_kv, a_w_in, a_conv_w, a_w_out, b_w_in, b_w_out, c_w_in, c_pool_w, c_pool_scale, c_w_out, ln_g, ln_b, ffn_w_gate, ffn_w_up, ffn_w_down, moe_router, moe_w_gate, moe_w_up, moe_w_down):
    bsz, seq_len, d = x.shape
    t = bsz * seq_len
    kt, v = _mem_kv(mem, w_mem_kv.astype(BF16))
    x = x.astype(F32)
    a_w_in, a_w_out, b_w_in, b_w_out, c_w_in, c_w_out = (
        w.astype(BF16) for w in (a_w_in, a_w_out, b_w_in, b_w_out, c_w_in, c_w_out))
    ffn_w = [w.astype(BF16) for w in (ffn_w_gate, ffn_w_up, ffn_w_down)]
    moe_w = [w.astype(BF16) for w in (moe_w_gate, moe_w_up, moe_w_down)]
    for i in range(DEPTH):
        kind, j = i % 3, i // 3
        g0, b0 = ln_g[i, 0].reshape(1, d), ln_b[i, 0].reshape(1, d)
        g1, b1 = ln_g[i, 1].reshape(1, d), ln_b[i, 1].reshape(1, d)
        if kind == 0:
            x = _mixer_layer(_conv_layer_body, j, x, a_w_in, [a_conv_w[j]], kt, v, a_w_out, g0, b0)
        elif kind == 1:
            h = _proj_in(j, x.reshape(t, d), b_w_in).reshape(bsz, seq_len, -1)
            a = _attention(h, kt, v)
            x = _proj_out_ln(j, a.reshape(t, -1), x.reshape(t, d), b_w_out, g0, b0)
            x = x.reshape(bsz, seq_len, d)
        else:
            x = _mixer_layer(_pool_layer_body, j, x, c_w_in,
                             [_block_diag(c_pool_w[j]).astype(BF16), c_pool_scale[j].reshape(1, -1)], kt, v,
                             c_w_out, g0, b0)
        f = i // 2
        x2 = x.reshape(t, d)
        if i % 2 == 0:
            x2 = _ffn_ln(f, x2, *ffn_w, g1, b1)
        else:
            x2 = _moe_ln(f, x2, moe_router[f], *moe_w, g1, b1)
        x = x2.reshape(bsz, seq_len, d)
    return x
```

```python
import functools

import jax
import jax.numpy as jnp
from jax import lax
from jax.experimental import pallas as pl
from jax.experimental.pallas import tpu as pltpu

F32 = jnp.float32
BF16 = jnp.bfloat16

D_MODEL = 1024
DEPTH = 4
HEAD_DIM = 64
MEM_HEADS = 4
MEM_WIDTH = MEM_HEADS * HEAD_DIM
MIX_WIDTH = 3 * D_MODEL // 4
DIL_GROUPS = ((128, 1), (512, 4), (2048, 16))
DIL_HEADS = 4
DIL_WIDTH = DIL_HEADS * HEAD_DIM
DIL_RADIUS = 64
POOL_WINDOWS = (2, 4, 8, 16)
POOL_GROUP = MIX_WIDTH // len(POOL_WINDOWS)
N_EXPERTS = 8
TOP_K = 2
LN_EPS = 1e-5
NEG_INF = -1e30
ALPHA = (2 * DEPTH) ** 0.25

LANES = 128
SUBLANES = 8
HALO = SUBLANES
SEQ_TILE = 1024
SUB_TILE = 512
ROW_TILE = 512
MATMUL_TILE = 1024
F_CHUNK = 768
DISPATCH_TILE = ROW_TILE
EXPERT_BLOCK = 512
EXPERT_F_TILES = 2
Q_BLOCK = 128
ISSUE_UNROLL = 8
VMEM_LIMIT = 56 * 1024 * 1024


def _params(*sem):
    return pltpu.CompilerParams(dimension_semantics=sem, vmem_limit_bytes=VMEM_LIMIT)


def _layer_norm(v, g, b):
    mu = jnp.mean(v, axis=-1, keepdims=True)
    d = v - mu
    var = jnp.mean(d * d, axis=-1, keepdims=True)
    return d * lax.rsqrt(var + LN_EPS) * g + b


def _dot(a, b):
    return jnp.dot(a, b, preferred_element_type=F32)


def _head_masks(width):
    col = lax.broadcasted_iota(jnp.int32, (1, width), 1)
    return [(col >= h * HEAD_DIM) & (col < (h + 1) * HEAD_DIM) for h in range(width // HEAD_DIM)]


def _stack_heads(q, masks):
    q = q * HEAD_DIM ** -0.5
    return jnp.concatenate([jnp.where(m, q, 0.0) for m in masks], axis=0).astype(BF16)


def _unstack_heads(o, masks, n):
    out = o[(len(masks) - 1) * n:]
    for h in range(len(masks) - 2, -1, -1):
        out = jnp.where(masks[h], o[h * n:(h + 1) * n], out)
    return out


def _mem_attention(q, kt, v):
    n = q.shape[0]
    masks = _head_masks(MEM_WIDTH)
    sc = _dot(_stack_heads(q, masks), kt)
    p = jnp.exp(sc - jnp.max(sc, axis=-1, keepdims=True))
    p = p / jnp.sum(p, axis=-1, keepdims=True)
    return _unstack_heads(_dot(p.astype(BF16), v), masks, n)


def _mem_kv_body(mem_ref, w_ref, kt_ref, v_ref):
    kv = _dot(mem_ref[...].astype(BF16), w_ref[...])
    kt_ref[...] = kv[:, :MEM_WIDTH].T.astype(BF16)
    v_ref[...] = kv[:, MEM_WIDTH:].astype(BF16)


def _mem_kv(mem, w_kv):
    b, m, d = mem.shape
    return pl.pallas_call(
        _mem_kv_body,
        grid=(b,),
        in_specs=[pl.BlockSpec((None, m, d), lambda i: (i, 0, 0)),
                  pl.BlockSpec((d, 2 * MEM_WIDTH), lambda i: (0, 0))],
        out_specs=[pl.BlockSpec((None, MEM_WIDTH, m), lambda i: (i, 0, 0)),
                   pl.BlockSpec((None, m, MEM_WIDTH), lambda i: (i, 0, 0))],
        out_shape=[jax.ShapeDtypeStruct((b, MEM_WIDTH, m), BF16),
                   jax.ShapeDtypeStruct((b, m, MEM_WIDTH), BF16)],
        compiler_params=_params("arbitrary"),
        name="mem_kv",
    )(mem, w_kv)


def _sub_tiles(xp_ref, xm_ref, xn_ref, seq_len):
    ts = xm_ref.shape[0]
    first = pl.program_id(1) * ts
    xm = xm_ref[...]
    xe = jnp.concatenate([xp_ref[...], xm, xn_ref[...]], axis=0).astype(BF16)
    pos = first - HALO + lax.broadcasted_iota(jnp.int32, (ts + 2 * HALO, 1), 0)
    valid = (pos >= 0) & (pos < seq_len)
    for r0 in range(0, ts, SUB_TILE):
        ext = slice(r0, r0 + SUB_TILE + 2 * HALO)
        yield r0, first + r0, xm[r0:r0 + SUB_TILE], xe[ext], valid[ext]


def _shift_rows(a, k):
    n = a.shape[0]
    return pltpu.roll(a, k % n, 0)


def _mixer_tail(mix, q_mem, xm, kt_ref, v_ref, wout_ref, g_ref, b_ref, o_ref, r0):
    mem_out = _mem_attention(q_mem, kt_ref[...], v_ref[...])
    y = (_dot(mix.astype(BF16), wout_ref[:MIX_WIDTH, :])
         + _dot(mem_out.astype(BF16), wout_ref[MIX_WIDTH:, :]))
    o_ref[r0:r0 + SUB_TILE, :] = _layer_norm(ALPHA * xm + y, g_ref[...], b_ref[...])


def _conv_layer_body(seq_len, xp_ref, xm_ref, xn_ref, win_ref, cw_ref, kt_ref, v_ref,
                     wout_ref, g_ref, b_ref, o_ref):
    n = SUB_TILE
    cw = cw_ref[...]
    for r0, _, xm, xe, valid in _sub_tiles(xp_ref, xm_ref, xn_ref, seq_len):
        xmb = xm.astype(BF16)
        gate_b = _dot(xmb, win_ref[:, :MIX_WIDTH])
        cu = _dot(xe, win_ref[:, MIX_WIDTH:3 * MIX_WIDTH])
        q_mem = _dot(xmb, win_ref[:, 3 * MIX_WIDTH:])
        z = jnp.where(valid, cu[:, :MIX_WIDTH] * cu[:, MIX_WIDTH:], 0.0)
        conv = (cw[0:1] * _shift_rows(z, 1)[HALO:HALO + n]
                + cw[1:2] * z[HALO:HALO + n]
                + cw[2:3] * _shift_rows(z, -1)[HALO:HALO + n])
        _mixer_tail(gate_b * conv, q_mem, xm, kt_ref, v_ref, wout_ref, g_ref, b_ref, o_ref, r0)


def _pool_layer_body(seq_len, xp_ref, xm_ref, xn_ref, win_ref, pw_ref, ps_ref, kt_ref, v_ref,
                     wout_ref, g_ref, b_ref, o_ref):
    n = SUB_TILE
    col = lax.broadcasted_iota(jnp.int32, (1, MIX_WIDTH), 1)
    for r0, first, xm, xe, valid in _sub_tiles(xp_ref, xm_ref, xn_ref, seq_len):
        u = jnp.where(valid, _dot(xe, win_ref[:, :MIX_WIDTH]), 0.0)
        q_mem = _dot(xm.astype(BF16), win_ref[:, MIX_WIDTH:])
        a2 = u + _shift_rows(u, 1)
        a4 = _shift_rows(a2, 1) + _shift_rows(a2, -1)
        a8 = _shift_rows(a4, 2) + _shift_rows(a4, -2)
        a16 = _shift_rows(a8, 4) + _shift_rows(a8, -4)
        pos = first + lax.broadcasted_iota(jnp.int32, (n, 1), 0)
        num = a16[HALO:HALO + n]
        cnt = None
        for gi in range(len(POOL_WINDOWS) - 1, -1, -1):
            w = POOL_WINDOWS[gi]
            c_w = (jnp.minimum(pos + (w // 2 - 1), seq_len - 1) - jnp.maximum(pos - w // 2, 0) + 1).astype(F32)
            if cnt is None:
                cnt = jnp.broadcast_to(c_w, (n, MIX_WIDTH))
            else:
                in_group = col < (gi + 1) * POOL_GROUP
                num = jnp.where(in_group, (a2, a4, a8)[gi][HALO:HALO + n], num)
                cnt = jnp.where(in_group, c_w, cnt)
        diff = num / cnt - u[HALO:HALO + n]
        mix = _dot(diff.astype(BF16), pw_ref[...]) * ps_ref[...]
        _mixer_tail(mix, q_mem, xm, kt_ref, v_ref, wout_ref, g_ref, b_ref, o_ref, r0)


def _whole(w, layer=None):
    if layer is None:
        return pl.BlockSpec(w.shape, lambda *_: (0,) * w.ndim)
    return pl.BlockSpec((None,) + w.shape[1:], lambda *_: (layer,) + (0,) * (w.ndim - 1))


def _mixer_layer(body, layer, x, w_in, extra, kt, v, w_out, g, b):
    bsz, seq_len, d = x.shape
    ts = SEQ_TILE
    n_halo_blocks = seq_len // HALO
    in_specs = [
        pl.BlockSpec((None, HALO, d), lambda i, s: (i, jnp.maximum(s * (ts // HALO) - 1, 0), 0)),
        pl.BlockSpec((None, ts, d), lambda i, s: (i, s, 0)),
        pl.BlockSpec((None, HALO, d), lambda i, s: (i, jnp.minimum((s + 1) * (ts // HALO), n_halo_blocks - 1), 0)),
        _whole(w_in, layer),
    ]
    in_specs += [_whole(e) for e in extra]
    in_specs += [
        pl.BlockSpec((None,) + kt.shape[1:], lambda i, s: (i, 0, 0)),
        pl.BlockSpec((None,) + v.shape[1:], lambda i, s: (i, 0, 0)),
        _whole(w_out, layer),
        _whole(g),
        _whole(b),
    ]
    return pl.pallas_call(
        functools.partial(body, seq_len),
        grid=(bsz, seq_len // ts),
        in_specs=in_specs,
        out_specs=pl.BlockSpec((None, ts, d), lambda i, s: (i, s, 0)),
        out_shape=jax.ShapeDtypeStruct(x.shape, F32),
        compiler_params=_params("arbitrary", "arbitrary"),
        name=body.__name__.strip("_"),
    )(x, x, x, w_in, *extra, kt, v, w_out, g, b)


def _proj_in_body(x_ref, w_ref, o_ref):
    o_ref[...] = _dot(x_ref[...].astype(BF16), w_ref[...])


def _proj_in(layer, x2, w):
    t, d = x2.shape
    n = w.shape[-1]
    return pl.pallas_call(
        _proj_in_body,
        grid=(t // MATMUL_TILE,),
        in_specs=[pl.BlockSpec((MATMUL_TILE, d), lambda i: (i, 0)),
                  _whole(w, layer)],
        out_specs=pl.BlockSpec((MATMUL_TILE, n), lambda i: (i, 0)),
        out_shape=jax.ShapeDtypeStruct((t, n), F32),
        compiler_params=_params("arbitrary"),
        name="proj_in",
    )(x2, w)


def _alibi_slope(index, total):
    return 2.0 ** (-8.0 * (index + 1) / total)


def _rows(start, size, stride):
    return pl.ds(start, size) if stride == 1 else pl.ds(start, size, stride=stride)


def _load_cols(refs, rows):
    return jnp.concatenate([r[rows, :] for r in refs], axis=1)


def _store_cols(refs, rows, val):
    for c, r in enumerate(refs):
        r[rows, :] = val[:, c * LANES:(c + 1) * LANES]


def _dilated_group(gi, dilation, seq_len, hq_refs, hk_refs, hv_refs, m_refs, l_refs, o_refs):
    n_sub = seq_len // dilation
    qb = min(Q_BLOCK, n_sub)
    kw = min(qb + 2 * DIL_RADIUS, n_sub)
    masks = _head_masks(DIL_WIDTH)
    n_heads_total = len(DIL_GROUPS) * DIL_HEADS
    bias_cache = {}

    def bias_and_mask(offset):
        if offset not in bias_cache:
            rel = offset + lax.broadcasted_iota(jnp.int32, (qb, kw), 0) - lax.broadcasted_iota(jnp.int32, (qb, kw), 1)
            dist = jnp.abs(rel)
            inside = dist <= DIL_RADIUS
            span = (dist * dilation).astype(F32)
            bias = jnp.concatenate(
                [-_alibi_slope(gi * DIL_HEADS + h, n_heads_total) * span for h in range(DIL_HEADS)], axis=0)
            bias_cache[offset] = (bias, jnp.concatenate([inside] * DIL_HEADS, axis=0))
        return bias_cache[offset]

    for r in range(dilation):
        for j0 in range(0, n_sub, qb):
            ks = min(max(j0 - DIL_RADIUS, 0), n_sub - kw)
            q_rows = _rows(r + j0 * dilation, qb, dilation)
            k_rows = _rows(r + ks * dilation, kw, dilation)
            q = _load_cols(hq_refs, q_rows)
            k = _load_cols(hk_refs, k_rows).astype(BF16)
            v = _load_cols(hv_refs, k_rows).astype(BF16)
            sc = lax.dot_general(_stack_heads(q, masks), k, (((1,), (1,)), ((), ())),
                                 preferred_element_type=F32)
            bias, inside = bias_and_mask(j0 - ks)
            sc = jnp.where(inside, sc + bias, NEG_INF)
            mx = jnp.max(sc, axis=-1, keepdims=True)
            p = jnp.exp(sc - mx)
            den = jnp.sum(p, axis=-1, keepdims=True)
            num = _unstack_heads(_dot(p.astype(BF16), v), masks, qb)
            mx = _unstack_heads(jnp.broadcast_to(mx, (DIL_HEADS * qb, DIL_WIDTH)), masks, qb)
            den = _unstack_heads(jnp.broadcast_to(den, (DIL_HEADS * qb, DIL_WIDTH)), masks, qb)
            if gi > 0:
                m_old = _load_cols(m_refs, q_rows)
                m_new = jnp.maximum(m_old, mx)
                a_old = jnp.exp(m_old - m_new)
                a_new = jnp.exp(mx - m_new)
                mx = m_new
                den = _load_cols(l_refs, q_rows) * a_old + den * a_new
                num = _load_cols(o_refs, q_rows) * a_old + num * a_new
            _store_cols(m_refs, q_rows, mx)
            _store_cols(l_refs, q_rows, den)
            _store_cols(o_refs, q_rows, num)


def _attention_body(*refs):
    n_slab = DIL_WIDTH // LANES
    hq_refs, hk_refs, hv_refs = (refs[i * n_slab:(i + 1) * n_slab] for i in range(3))
    hm_ref, kt_ref, v_ref, out_ref = refs[3 * n_slab:3 * n_slab + 4]
    m_refs, l_refs, o_refs = (refs[3 * n_slab + 4 + i * n_slab:3 * n_slab + 4 + (i + 1) * n_slab] for i in range(3))
    g = pl.program_id(1)
    seq_len = hm_ref.shape[0]

    @pl.when(g == 0)
    def _():
        for c in range(0, seq_len, ROW_TILE):
            mem_out = _mem_attention(hm_ref[c:c + ROW_TILE, :], kt_ref[...], v_ref[...])
            out_ref[c:c + ROW_TILE, DIL_WIDTH:] = mem_out.astype(BF16)

    for gi, (_, dilation) in enumerate(DIL_GROUPS):
        @pl.when(g == gi)
        def _(gi=gi, dilation=dilation):
            _dilated_group(gi, dilation, seq_len, hq_refs, hk_refs, hv_refs, m_refs, l_refs, o_refs)

    @pl.when(g == len(DIL_GROUPS) - 1)
    def _():
        for c in range(n_slab):
            out_ref[:, c * LANES:(c + 1) * LANES] = (o_refs[c][...] / l_refs[c][...]).astype(BF16)


def _attention(h, kt, v):
    bsz, seq_len, _ = h.shape
    n_g = len(DIL_GROUPS)
    n_slab = DIL_WIDTH // LANES

    def slabs(base):
        return [pl.BlockSpec((None, seq_len, LANES), lambda i, g, c=c: (i, 0, (base + g) * n_slab + c))
                for c in range(n_slab)]

    return pl.pallas_call(
        _attention_body,
        grid=(bsz, n_g),
        in_specs=slabs(0) + slabs(n_g) + slabs(2 * n_g) + [
            pl.BlockSpec((None, seq_len, MEM_WIDTH), lambda i, g: (i, 0, 3 * n_g)),
            pl.BlockSpec((None,) + kt.shape[1:], lambda i, g: (i, 0, 0)),
            pl.BlockSpec((None,) + v.shape[1:], lambda i, g: (i, 0, 0))],
        out_specs=pl.BlockSpec((None, seq_len, DIL_WIDTH + MEM_WIDTH), lambda i, g: (i, 0, 0)),
        out_shape=jax.ShapeDtypeStruct((bsz, seq_len, DIL_WIDTH + MEM_WIDTH), BF16),
        scratch_shapes=[pltpu.VMEM((seq_len, LANES), F32)] * (3 * n_slab),
        compiler_params=_params("arbitrary", "arbitrary"),
        name="dilated_attention",
    )(*([h] * (3 * n_slab + 1)), kt, v)


def _proj_out_ln_body(a_ref, x_ref, w_ref, g_ref, b_ref, o_ref):
    y = _dot(a_ref[...], w_ref[...])
    o_ref[...] = _layer_norm(ALPHA * x_ref[...] + y, g_ref[...], b_ref[...])


def _proj_out_ln(layer, a2, x2, w, g, b):
    t, d = x2.shape
    k = a2.shape[1]
    return pl.pallas_call(
        _proj_out_ln_body,
        grid=(t // ROW_TILE,),
        in_specs=[pl.BlockSpec((ROW_TILE, k), lambda i: (i, 0)),
                  pl.BlockSpec((ROW_TILE, d), lambda i: (i, 0)),
                  _whole(w, layer), _whole(g), _whole(b)],
        out_specs=pl.BlockSpec((ROW_TILE, d), lambda i: (i, 0)),
        out_shape=jax.ShapeDtypeStruct((t, d), F32),
        compiler_params=_params("arbitrary"),
        name="proj_out_ln",
    )(a2, x2, w, g, b)


def _swiglu(xb, wg_ref, wu_ref, wd_ref, f_chunk):
    y = None
    f = wg_ref.shape[1]
    for c in range(0, f, f_chunk):
        e = min(c + f_chunk, f)
        g = _dot(xb, wg_ref[:, c:e])
        u = _dot(xb, wu_ref[:, c:e])
        part = _dot((g * jax.nn.sigmoid(g) * u).astype(BF16), wd_ref[c:e, :])
        y = part if y is None else y + part
    return y


def _ffn_ln_body(x_ref, wg_ref, wu_ref, wd_ref, g_ref, b_ref, o_ref):
    x = x_ref[...]
    y = _swiglu(x.astype(BF16), wg_ref, wu_ref, wd_ref, F_CHUNK)
    o_ref[...] = _layer_norm(ALPHA * x + y, g_ref[...], b_ref[...])


def _ffn_ln(layer, x2, wg, wu, wd, g, b):
    t, d = x2.shape
    f = wg.shape[-1]
    resident = lambda w: pl.BlockSpec((None,) + w.shape[1:], lambda i: (layer, 0, 0), pipeline_mode=pl.Buffered(1))
    return pl.pallas_call(
        _ffn_ln_body,
        grid=(t // MATMUL_TILE,),
        in_specs=[pl.BlockSpec((MATMUL_TILE, d), lambda i: (i, 0)),
                  resident(wg), resident(wu), resident(wd), _whole(g), _whole(b)],
        out_specs=pl.BlockSpec((MATMUL_TILE, d), lambda i: (i, 0)),
        out_shape=jax.ShapeDtypeStruct((t, d), F32),
        compiler_params=_params("arbitrary"),
        name="ffn_ln",
    )(x2, wg, wu, wd, g, b)


_COL_E0, _COL_E1, _COL_R0, _COL_R1, _COL_W0, _COL_W1 = range(6)


def _split_bf16(a):
    hi = a.astype(BF16)
    return hi, (a - hi.astype(F32)).astype(BF16)


def _router_body(x_ref, whl_ref, tri_ref, meta_ref, route_ref, cnt_ref, carry_ref):
    i = pl.program_id(0)

    @pl.when(i == 0)
    def _():
        carry_ref[...] = jnp.zeros_like(carry_ref)

    tm = x_ref.shape[0]
    lane = lax.broadcasted_iota(jnp.int32, (tm, LANES), 1).astype(F32)
    xh, xl = _split_bf16(x_ref[...])
    both = _dot(xh, whl_ref[...])
    logits = both[:, :LANES] + (_dot(xl, whl_ref[:, :LANES]) + both[:, LANES:])
    logits = jnp.where(lane < N_EXPERTS, logits, -jnp.inf)
    m0 = jnp.max(logits, axis=-1, keepdims=True)
    e0 = jnp.min(jnp.where(logits == m0, lane, float(LANES)), axis=-1, keepdims=True)
    rest = jnp.where(lane == e0, -jnp.inf, logits)
    m1 = jnp.max(rest, axis=-1, keepdims=True)
    e1 = jnp.min(jnp.where(rest == m1, lane, float(LANES)), axis=-1, keepdims=True)
    ex = jnp.exp(m1 - m0)
    w0 = 1.0 / (1.0 + ex)
    w1 = ex / (1.0 + ex)
    hit0 = lane == e0
    hit1 = lane == e1
    onehot = (hit0 | hit1).astype(F32)
    before = _dot(tri_ref[...], onehot.astype(BF16)) + carry_ref[...]
    r0 = jnp.sum(jnp.where(hit0, before, 0.0), axis=-1, keepdims=True)
    r1 = jnp.sum(jnp.where(hit1, before, 0.0), axis=-1, keepdims=True)
    carry_ref[...] += jnp.sum(onehot, axis=0, keepdims=True)
    cnt_ref[...] = carry_ref[...]
    meta = jnp.zeros((tm, LANES), F32)
    for col, val in ((_COL_E0, e0), (_COL_E1, e1), (_COL_R0, r0), (_COL_R1, r1), (_COL_W0, w0), (_COL_W1, w1)):
        meta = jnp.where(lane == col, val, meta)
    meta_ref[...] = meta
    route_ref[...] = meta.T[:SUBLANES, :]


def _router(x2, w_router):
    t, d = x2.shape
    whl = jnp.concatenate(_split_bf16(jnp.zeros((d, LANES), F32).at[:, :N_EXPERTS].set(w_router)), axis=1)
    tri = jnp.tril(jnp.ones((ROW_TILE, ROW_TILE), BF16), -1)
    return pl.pallas_call(
        _router_body,
        grid=(t // ROW_TILE,),
        in_specs=[pl.BlockSpec((ROW_TILE, d), lambda i: (i, 0)), _whole(whl), _whole(tri)],
        out_specs=[pl.BlockSpec((ROW_TILE, LANES), lambda i: (i, 0)),
                   pl.BlockSpec((None, SUBLANES, ROW_TILE), lambda i: (i, 0, 0)),
                   pl.BlockSpec((1, LANES), lambda i: (0, 0))],
        out_shape=[jax.ShapeDtypeStruct((t, LANES), F32),
                   jax.ShapeDtypeStruct((t // ROW_TILE, SUBLANES, ROW_TILE), F32),
                   jax.ShapeDtypeStruct((1, LANES), F32)],
        scratch_shapes=[pltpu.VMEM((1, LANES), F32)],
        compiler_params=_params("arbitrary"),
        name="router",
    )(x2, whl, tri)


assert D_MODEL == SUBLANES * LANES


def _store_row_tiles(ref, val):
    n = val.shape[0]
    for c in range(SUBLANES):
        ref[pl.ds(c, n, stride=SUBLANES), :] = val[:, c * LANES:(c + 1) * LANES]


def _load_row_tiles(ref, n):
    return jnp.concatenate([ref[pl.ds(c, n, stride=SUBLANES), :] for c in range(SUBLANES)], axis=1)


def _tile_of(ref, row):
    return ref.at[pl.ds(pl.multiple_of(row * SUBLANES, SUBLANES), SUBLANES), :]


def _wait_rows(hbm_rows_ref, n, sem):
    span = hbm_rows_ref.at[pl.ds(0, n)]
    pltpu.make_async_copy(span, span, sem).wait()


def _issue_row_copies(n_tokens, copy_of):
    def group(gidx, _):
        for u in range(ISSUE_UNROLL):
            for k in range(TOP_K):
                copy_of(gidx * ISSUE_UNROLL + u, k).start(priority=k)
        return 0

    lax.fori_loop(0, n_tokens // ISSUE_UNROLL, group, 0)


def _dispatch_body(pad_end_ref, dest_ref, x_ref, xg_ref, stage_ref, zero_ref, sem, zero_sem):
    tm = x_ref.shape[0]

    @pl.when(pl.program_id(0) == 0)
    def _():
        zero_ref[...] = jnp.zeros_like(zero_ref)

        def fill(start):
            copy = pltpu.make_async_copy(zero_ref, xg_ref.at[pl.ds(start, EXPERT_BLOCK)], zero_sem)
            copy.start()
            copy.wait()

        for e in range(N_EXPERTS):
            fill(jnp.maximum(pad_end_ref[e] - EXPERT_BLOCK, 0))
            tail = pad_end_ref[N_EXPERTS - 1] + e * EXPERT_BLOCK
            pl.when(tail < xg_ref.shape[0])(functools.partial(fill, tail))

    _store_row_tiles(stage_ref, x_ref[...])
    for part in range(tm // ROW_TILE):
        _issue_row_copies(ROW_TILE, lambda tok, k, part=part: pltpu.make_async_copy(
            _tile_of(stage_ref, part * ROW_TILE + tok), xg_ref.at[dest_ref[part, 0, k * ROW_TILE + tok]], sem))
    for _ in range(TOP_K):
        _wait_rows(xg_ref, tm, sem)


def _dispatch(x2, dest, pad_end, n_rows):
    t, d = x2.shape
    tm = DISPATCH_TILE
    n_tiles = t // tm
    grid_spec = pltpu.PrefetchScalarGridSpec(
        num_scalar_prefetch=1,
        grid=(n_tiles,),
        in_specs=[pl.BlockSpec((tm // ROW_TILE, 1, TOP_K * ROW_TILE), lambda i, pe: (i, 0, 0),
                               memory_space=pltpu.SMEM),
                  pl.BlockSpec((tm, d), lambda i, pe: (i, 0))],
        out_specs=pl.BlockSpec(memory_space=pl.ANY),
        scratch_shapes=[pltpu.VMEM((tm * SUBLANES, LANES), F32),
                        pltpu.VMEM((EXPERT_BLOCK, SUBLANES, LANES), F32),
                        pltpu.SemaphoreType.DMA(()), pltpu.SemaphoreType.DMA(())],
    )
    return pl.pallas_call(
        _dispatch_body,
        grid_spec=grid_spec,
        out_shape=jax.ShapeDtypeStruct((n_rows, SUBLANES, LANES), F32),
        compiler_params=_params("arbitrary"),
        name="dispatch",
    )(pad_end, dest, x2)


def _experts_body(be_ref, nused_ref, x_ref, wg_ref, wu_ref, wd_ref, o_ref, acc_ref, xb_ref):
    blk = pl.program_id(0)
    j = pl.program_id(1)

    @pl.when((blk < nused_ref[0]) & (j == 0))
    def _():
        xb = _load_row_tiles(x_ref, EXPERT_BLOCK).astype(BF16)
        xb_ref[...] = xb
        acc_ref[...] = _swiglu(xb, wg_ref, wu_ref, wd_ref, F_CHUNK)

    @pl.when((blk < nused_ref[0]) & (j == EXPERT_F_TILES - 1))
    def _():
        _store_row_tiles(o_ref, acc_ref[...] + _swiglu(xb_ref[...], wg_ref, wu_ref, wd_ref, F_CHUNK))

    @pl.when(blk >= nused_ref[0])
    def _():
        o_ref[...] = jnp.zeros_like(o_ref)


def _experts(layer, xg, block_e, n_used, wg, wu, wd):
    assert EXPERT_F_TILES == 2
    n_rows = xg.shape[0] // SUBLANES
    d = D_MODEL
    f = wg.shape[-1]
    ft = f // EXPERT_F_TILES
    n_blocks = n_rows // EXPERT_BLOCK
    last = EXPERT_F_TILES - 1
    tile_rows = EXPERT_BLOCK * SUBLANES

    def used(blk, nu):
        return jnp.maximum(jnp.minimum(blk, nu[0] - 1), 0)

    def f_tile(blk, j, nu):
        return jnp.where(blk < nu[0], j, last)

    grid_spec = pltpu.PrefetchScalarGridSpec(
        num_scalar_prefetch=2,
        grid=(n_blocks, EXPERT_F_TILES),
        in_specs=[pl.BlockSpec((tile_rows, LANES), lambda blk, j, be, nu: (used(blk, nu), 0)),
                  pl.BlockSpec((None, None, d, ft),
                               lambda blk, j, be, nu: (layer, be[used(blk, nu)], 0, f_tile(blk, j, nu))),
                  pl.BlockSpec((None, None, d, ft),
                               lambda blk, j, be, nu: (layer, be[used(blk, nu)], 0, f_tile(blk, j, nu))),
                  pl.BlockSpec((None, None, ft, d),
                               lambda blk, j, be, nu: (layer, be[used(blk, nu)], f_tile(blk, j, nu), 0))],
        out_specs=pl.BlockSpec((tile_rows, LANES), lambda blk, j, be, nu: (blk, 0)),
        scratch_shapes=[pltpu.VMEM((EXPERT_BLOCK, d), F32), pltpu.VMEM((EXPERT_BLOCK, d), BF16)],
    )
    return pl.pallas_call(
        _experts_body,
        grid_spec=grid_spec,
        out_shape=jax.ShapeDtypeStruct(xg.shape, F32),
        compiler_params=_params("arbitrary", "arbitrary"),
        name="experts",
    )(block_e, n_used, xg, wg, wu, wd)


def _combine_ln_body(dest_ref, dest_next_ref, x_ref, w_ref, yb_ref, g_ref, b_ref, o_ref, buf_ref, sem):
    i = pl.program_id(0)
    tm = x_ref.shape[0]
    slot = lax.rem(i, 2)

    def gather(idx_ref, into):
        _issue_row_copies(tm, lambda tok, k: pltpu.make_async_copy(
            yb_ref.at[idx_ref[0, 0, k * tm + tok]], _tile_of(buf_ref.at[into, k], tok), sem.at[into]))

    @pl.when(i == 0)
    def _():
        gather(dest_ref, 0)

    @pl.when(i + 1 < pl.num_programs(0))
    def _():
        gather(dest_next_ref, 1 - slot)

    for _ in range(TOP_K):
        _wait_rows(yb_ref, tm, sem.at[slot])
    w = w_ref[...]
    y = (w[:, _COL_W0:_COL_W0 + 1] * _load_row_tiles(buf_ref.at[slot, 0], tm)
         + w[:, _COL_W1:_COL_W1 + 1] * _load_row_tiles(buf_ref.at[slot, 1], tm))
    o_ref[...] = _layer_norm(ALPHA * x_ref[...] + y, g_ref[...], b_ref[...])


def _combine_ln(x2, meta, yb, dest, g, b):
    t, d = x2.shape
    n_tiles = t // ROW_TILE
    dest_block = lambda index_map: pl.BlockSpec((1, 1, TOP_K * ROW_TILE), index_map, memory_space=pltpu.SMEM)
    return pl.pallas_call(
        _combine_ln_body,
        grid=(n_tiles,),
        in_specs=[dest_block(lambda i: (i, 0, 0)),
                  dest_block(lambda i: (jnp.minimum(i + 1, n_tiles - 1), 0, 0)),
                  pl.BlockSpec((ROW_TILE, d), lambda i: (i, 0)),
                  pl.BlockSpec((ROW_TILE, LANES), lambda i: (i, 0)),
                  pl.BlockSpec(memory_space=pl.ANY),
                  _whole(g), _whole(b)],
        out_specs=pl.BlockSpec((ROW_TILE, d), lambda i: (i, 0)),
        out_shape=jax.ShapeDtypeStruct((t, d), F32),
        scratch_shapes=[pltpu.VMEM((2, TOP_K, ROW_TILE * SUBLANES, LANES), F32), pltpu.SemaphoreType.DMA((2,))],
        compiler_params=_params("arbitrary"),
        name="combine_ln",
    )(dest, dest, x2, meta, yb, g, b)


def _moe_ln(layer, x2, w_router, wg, wu, wd, g, b):
    t, _ = x2.shape
    meta, routes, counts = _router(x2, w_router)
    counts = counts[0, :N_EXPERTS].astype(jnp.int32)
    padded = (counts + EXPERT_BLOCK - 1) // EXPERT_BLOCK * EXPERT_BLOCK
    pad_end = jnp.cumsum(padded)
    pad_start = pad_end - padded
    experts = routes[:, _COL_E0:_COL_E1 + 1, :].astype(jnp.int32)
    ranks = routes[:, _COL_R0:_COL_R1 + 1, :].astype(jnp.int32)
    dest = ranks
    for e in range(N_EXPERTS):
        dest = dest + jnp.where(experts == e, pad_start[e], 0)
    n_blocks = t * TOP_K // EXPERT_BLOCK + N_EXPERTS
    block_start = jnp.arange(n_blocks, dtype=jnp.int32) * EXPERT_BLOCK
    block_e = jnp.minimum(jnp.sum(pad_end[None, :] <= block_start[:, None], axis=1), N_EXPERTS - 1).astype(jnp.int32)
    n_used = (pad_end[-1:] // EXPERT_BLOCK).astype(jnp.int32)
    dest = dest.reshape(dest.shape[0], 1, TOP_K * ROW_TILE)
    n_rows = n_blocks * EXPERT_BLOCK
    xg = _dispatch(x2, dest, pad_end.astype(jnp.int32), n_rows)
    yb = _experts(layer, xg.reshape(n_rows * SUBLANES, LANES), block_e, n_used, wg, wu, wd)
    return _combine_ln(x2, meta, yb.reshape(n_rows, SUBLANES, LANES), dest, g, b)


def _block_diag(w):
    n_g, c, _ = w.shape
    out = jnp.zeros((n_g * c, n_g * c), w.dtype)
    for gi in range(n_g):
        out = out.at[gi * c:(gi + 1) * c, gi * c:(gi + 1) * c].set(w[gi])
    return out


def kernel(x, mem, w_mem_kv, a_w_in, a_conv_w, a_w_out, b_w_in, b_w_out, c_w_in, c_pool_w, c_pool_scale, c_w_out, ln_g, ln_b, ffn_w_gate, ffn_w_up, ffn_w_down, moe_router, moe_w_gate, moe_w_up, moe_w_down):
    bsz, seq_len, d = x.shape
    t = bsz * seq_len
    kt, v = _mem_kv(mem, w_mem_kv.astype(BF16))
    x = x.astype(F32)
    a_w_in, a_w_out, b_w_in, b_w_out, c_w_in, c_w_out = (
        w.astype(BF16) for w in (a_w_in, a_w_out, b_w_in, b_w_out, c_w_in, c_w_out))
    ffn_w = [w.astype(BF16) for w in (ffn_w_gate, ffn_w_up, ffn_w_down)]
    moe_w = [w.astype(BF16) for w in (moe_w_gate, moe_w_up, moe_w_down)]
    for i in range(DEPTH):
        kind, j = i % 3, i // 3
        g0, b0 = ln_g[i, 0].reshape(1, d), ln_b[i, 0].reshape(1, d)
        g1, b1 = ln_g[i, 1].reshape(1, d), ln_b[i, 1].reshape(1, d)
        if kind == 0:
            x = _mixer_layer(_conv_layer_body, j, x, a_w_in, [a_conv_w[j]], kt, v, a_w_out, g0, b0)
        elif kind == 1:
            h = _proj_in(j, x.reshape(t, d), b_w_in).reshape(bsz, seq_len, -1)
            a = _attention(h, kt, v)
            x = _proj_out_ln(j, a.reshape(t, -1), x.reshape(t, d), b_w_out, g0, b0)
            x = x.reshape(bsz, seq_len, d)
        else:
            x = _mixer_layer(_pool_layer_body, j, x, c_w_in,
                             [_block_diag(c_pool_w[j]).astype(BF16), c_pool_scale[j].reshape(1, -1)], kt, v,
                             c_w_out, g0, b0)
        f = i // 2
        x2 = x.reshape(t, d)
        if i % 2 == 0:
            x2 = _ffn_ln(f, x2, *ffn_w, g1, b1)
        else:
            x2 = _moe_ln(f, x2, moe_router[f], *moe_w, g1, b1)
        x = x2.reshape(bsz, seq_len, d)
    return x
```

```python
import functools

import jax
import jax.numpy as jnp
from jax import lax
from jax.experimental import pallas as pl
from jax.experimental.pallas import tpu as pltpu

F32 = jnp.float32
BF16 = jnp.bfloat16

D_MODEL = 1024
DEPTH = 4
HEAD_DIM = 64
MEM_HEADS = 4
MEM_WIDTH = MEM_HEADS * HEAD_DIM
MIX_WIDTH = 3 * D_MODEL // 4
DIL_GROUPS = ((128, 1), (512, 4), (2048, 16))
DIL_HEADS = 4
DIL_WIDTH = DIL_HEADS * HEAD_DIM
DIL_RADIUS = 64
POOL_WINDOWS = (2, 4, 8, 16)
POOL_GROUP = MIX_WIDTH // len(POOL_WINDOWS)
N_EXPERTS = 8
TOP_K = 2
LN_EPS = 1e-5
NEG_INF = -1e30
ALPHA = (2 * DEPTH) ** 0.25

LANES = 128
SUBLANES = 8
HALO = SUBLANES
SEQ_TILE = 1024
SUB_TILE = 512
ROW_TILE = 512
MATMUL_TILE = 1024
F_CHUNK = 768
DISPATCH_TILE = ROW_TILE
EXPERT_BLOCK = 512
EXPERT_F_TILES = 2
Q_BLOCK = 128
ISSUE_UNROLL = 8
VMEM_LIMIT = 56 * 1024 * 1024


def _params(*sem):
    return pltpu.CompilerParams(dimension_semantics=sem, vmem_limit_bytes=VMEM_LIMIT)


def _layer_norm(v, g, b):
    mu = jnp.mean(v, axis=-1, keepdims=True)
    d = v - mu
    var = jnp.mean(d * d, axis=-1, keepdims=True)
    return d * lax.rsqrt(var + LN_EPS) * g + b


def _dot(a, b):
    return jnp.dot(a, b, preferred_element_type=F32)


def _head_masks(width):
    col = lax.broadcasted_iota(jnp.int32, (1, width), 1)
    return [(col >= h * HEAD_DIM) & (col < (h + 1) * HEAD_DIM) for h in range(width // HEAD_DIM)]


def _stack_heads(q, masks):
    q = q * HEAD_DIM ** -0.5
    return jnp.concatenate([jnp.where(m, q, 0.0) for m in masks], axis=0).astype(BF16)


def _unstack_heads(o, masks, n):
    out = o[(len(masks) - 1) * n:]
    for h in range(len(masks) - 2, -1, -1):
        out = jnp.where(masks[h], o[h * n:(h + 1) * n], out)
    return out


def _mem_attention(q, kt, v):
    n = q.shape[0]
    masks = _head_masks(MEM_WIDTH)
    sc = _dot(_stack_heads(q, masks), kt)
    p = jnp.exp(sc - jnp.max(sc, axis=-1, keepdims=True))
    p = p / jnp.sum(p, axis=-1, keepdims=True)
    return _unstack_heads(_dot(p.astype(BF16), v), masks, n)


def _mem_kv_body(mem_ref, w_ref, kt_ref, v_ref):
    kv = _dot(mem_ref[...].astype(BF16), w_ref[...])
    kt_ref[...] = kv[:, :MEM_WIDTH].T.astype(BF16)
    v_ref[...] = kv[:, MEM_WIDTH:].astype(BF16)


def _mem_kv(mem, w_kv):
    b, m, d = mem.shape
    return pl.pallas_call(
        _mem_kv_body,
        grid=(b,),
        in_specs=[pl.BlockSpec((None, m, d), lambda i: (i, 0, 0)),
                  pl.BlockSpec((d, 2 * MEM_WIDTH), lambda i: (0, 0))],
        out_specs=[pl.BlockSpec((None, MEM_WIDTH, m), lambda i: (i, 0, 0)),
                   pl.BlockSpec((None, m, MEM_WIDTH), lambda i: (i, 0, 0))],
        out_shape=[jax.ShapeDtypeStruct((b, MEM_WIDTH, m), BF16),
                   jax.ShapeDtypeStruct((b, m, MEM_WIDTH), BF16)],
        compiler_params=_params("arbitrary"),
        name="mem_kv",
    )(mem, w_kv)


def _sub_tiles(xp_ref, xm_ref, xn_ref, seq_len):
    ts = xm_ref.shape[0]
    first = pl.program_id(1) * ts
    xm = xm_ref[...]
    xe = jnp.concatenate([xp_ref[...], xm, xn_ref[...]], axis=0).astype(BF16)
    pos = first - HALO + lax.broadcasted_iota(jnp.int32, (ts + 2 * HALO, 1), 0)
    valid = (pos >= 0) & (pos < seq_len)
    for r0 in range(0, ts, SUB_TILE):
        ext = slice(r0, r0 + SUB_TILE + 2 * HALO)
        yield r0, first + r0, xm[r0:r0 + SUB_TILE], xe[ext], valid[ext]


def _shift_rows(a, k):
    n = a.shape[0]
    return pltpu.roll(a, k % n, 0)


def _mixer_tail(mix, q_mem, xm, kt_ref, v_ref, wout_ref, g_ref, b_ref, o_ref, r0):
    mem_out = _mem_attention(q_mem, kt_ref[...], v_ref[...])
    y = (_dot(mix.astype(BF16), wout_ref[:MIX_WIDTH, :])
         + _dot(mem_out.astype(BF16), wout_ref[MIX_WIDTH:, :]))
    o_ref[r0:r0 + SUB_TILE, :] = _layer_norm(ALPHA * xm + y, g_ref[...], b_ref[...])


def _conv_layer_body(seq_len, xp_ref, xm_ref, xn_ref, win_ref, cw_ref, kt_ref, v_ref,
                     wout_ref, g_ref, b_ref, o_ref):
    n = SUB_TILE
    cw = cw_ref[...]
    for r0, _, xm, xe, valid in _sub_tiles(xp_ref, xm_ref, xn_ref, seq_len):
        xmb = xm.astype(BF16)
        gate_b = _dot(xmb, win_ref[:, :MIX_WIDTH])
        cu = _dot(xe, win_ref[:, MIX_WIDTH:3 * MIX_WIDTH])
        q_mem = _dot(xmb, win_ref[:, 3 * MIX_WIDTH:])
        z = jnp.where(valid, cu[:, :MIX_WIDTH] * cu[:, MIX_WIDTH:], 0.0)
        conv = (cw[0:1] * _shift_rows(z, 1)[HALO:HALO + n]
                + cw[1:2] * z[HALO:HALO + n]
                + cw[2:3] * _shift_rows(z, -1)[HALO:HALO + n])
        _mixer_tail(gate_b * conv, q_mem, xm, kt_ref, v_ref, wout_ref, g_ref, b_ref, o_ref, r0)


def _pool_layer_body(seq_len, xp_ref, xm_ref, xn_ref, win_ref, pw_ref, ps_ref, kt_ref, v_ref,
                     wout_ref, g_ref, b_ref, o_ref):
    n = SUB_TILE
    col = lax.broadcasted_iota(jnp.int32, (1, MIX_WIDTH), 1)
    for r0, first, xm, xe, valid in _sub_tiles(xp_ref, xm_ref, xn_ref, seq_len):
        u = jnp.where(valid, _dot(xe, win_ref[:, :MIX_WIDTH]), 0.0)
        q_mem = _dot(xm.astype(BF16), win_ref[:, MIX_WIDTH:])
        a2 = u + _shift_rows(u, 1)
        a4 = _shift_rows(a2, 1) + _shift_rows(a2, -1)
        a8 = _shift_rows(a4, 2) + _shift_rows(a4, -2)
        a16 = _shift_rows(a8, 4) + _shift_rows(a8, -4)
        pos = first + lax.broadcasted_iota(jnp.int32, (n, 1), 0)
        num = a16[HALO:HALO + n]
        cnt = None
        for gi in range(len(POOL_WINDOWS) - 1, -1, -1):
            w = POOL_WINDOWS[gi]
            c_w = (jnp.minimum(pos + (w // 2 - 1), seq_len - 1) - jnp.maximum(pos - w // 2, 0) + 1).astype(F32)
            if cnt is None:
                cnt = jnp.broadcast_to(c_w, (n, MIX_WIDTH))
            else:
                in_group = col < (gi + 1) * POOL_GROUP
                num = jnp.where(in_group, (a2, a4, a8)[gi][HALO:HALO + n], num)
                cnt = jnp.where(in_group, c_w, cnt)
        diff = num / cnt - u[HALO:HALO + n]
        mix = _dot(diff.astype(BF16), pw_ref[...]) * ps_ref[...]
        _mixer_tail(mix, q_mem, xm, kt_ref, v_ref, wout_ref, g_ref, b_ref, o_ref, r0)


def _whole(w, layer=None):
    if layer is None:
        return pl.BlockSpec(w.shape, lambda *_: (0,) * w.ndim)
    return pl.BlockSpec((None,) + w.shape[1:], lambda *_: (layer,) + (0,) * (w.ndim - 1))


def _mixer_layer(body, layer, x, w_in, extra, kt, v, w_out, g, b):
    bsz, seq_len, d = x.shape
    ts = SEQ_TILE
    n_halo_blocks = seq_len // HALO
    in_specs = [
        pl.BlockSpec((None, HALO, d), lambda i, s: (i, jnp.maximum(s * (ts // HALO) - 1, 0), 0)),
        pl.BlockSpec((None, ts, d), lambda i, s: (i, s, 0)),
        pl.BlockSpec((None, HALO, d), lambda i, s: (i, jnp.minimum((s + 1) * (ts // HALO), n_halo_blocks - 1), 0)),
        _whole(w_in, layer),
    ]
    in_specs += [_whole(e) for e in extra]
    in_specs += [
        pl.BlockSpec((None,) + kt.shape[1:], lambda i, s: (i, 0, 0)),
        pl.BlockSpec((None,) + v.shape[1:], lambda i, s: (i, 0, 0)),
        _whole(w_out, layer),
        _whole(g),
        _whole(b),
    ]
    return pl.pallas_call(
        functools.partial(body, seq_len),
        grid=(bsz, seq_len // ts),
        in_specs=in_specs,
        out_specs=pl.BlockSpec((None, ts, d), lambda i, s: (i, s, 0)),
        out_shape=jax.ShapeDtypeStruct(x.shape, F32),
        compiler_params=_params("arbitrary", "arbitrary"),
        name=body.__name__.strip("_"),
    )(x, x, x, w_in, *extra, kt, v, w_out, g, b)


def _proj_in_body(x_ref, w_ref, o_ref):
    o_ref[...] = _dot(x_ref[...].astype(BF16), w_ref[...])


def _proj_in(layer, x2, w):
    t, d = x2.shape
    n = w.shape[-1]
    return pl.pallas_call(
        _proj_in_body,
        grid=(t // MATMUL_TILE,),
        in_specs=[pl.BlockSpec((MATMUL_TILE, d), lambda i: (i, 0)),
                  _whole(w, layer)],
        out_specs=pl.BlockSpec((MATMUL_TILE, n), lambda i: (i, 0)),
        out_shape=jax.ShapeDtypeStruct((t, n), F32),
        compiler_params=_params("arbitrary"),
        name="proj_in",
    )(x2, w)


def _alibi_slope(index, total):
    return 2.0 ** (-8.0 * (index + 1) / total)


def _rows(start, size, stride):
    return pl.ds(start, size) if stride == 1 else pl.ds(start, size, stride=stride)


def _load_cols(refs, rows):
    return jnp.concatenate([r[rows, :] for r in refs], axis=1)


def _store_cols(refs, rows, val):
    for c, r in enumerate(refs):
        r[rows, :] = val[:, c * LANES:(c + 1) * LANES]


def _dilated_group(gi, dilation, seq_len, hq_refs, hk_refs, hv_refs, m_refs, l_refs, o_refs):
    n_sub = seq_len // dilation
    qb = min(Q_BLOCK, n_sub)
    kw = min(qb + 2 * DIL_RADIUS, n_sub)
    masks = _head_masks(DIL_WIDTH)
    n_heads_total = len(DIL_GROUPS) * DIL_HEADS
    bias_cache = {}

    def bias_and_mask(offset):
        if offset not in bias_cache:
            rel = offset + lax.broadcasted_iota(jnp.int32, (qb, kw), 0) - lax.broadcasted_iota(jnp.int32, (qb, kw), 1)
            dist = jnp.abs(rel)
            inside = dist <= DIL_RADIUS
            span = (dist * dilation).astype(F32)
            bias = jnp.concatenate(
                [-_alibi_slope(gi * DIL_HEADS + h, n_heads_total) * span for h in range(DIL_HEADS)], axis=0)
            bias_cache[offset] = (bias, jnp.concatenate([inside] * DIL_HEADS, axis=0))
        return bias_cache[offset]

    for r in range(dilation):
        for j0 in range(0, n_sub, qb):
            ks = min(max(j0 - DIL_RADIUS, 0), n_sub - kw)
            q_rows = _rows(r + j0 * dilation, qb, dilation)
            k_rows = _rows(r + ks * dilation, kw, dilation)
            q = _load_cols(hq_refs, q_rows)
            k = _load_cols(hk_refs, k_rows).astype(BF16)
            v = _load_cols(hv_refs, k_rows).astype(BF16)
            sc = lax.dot_general(_stack_heads(q, masks), k, (((1,), (1,)), ((), ())),
                                 preferred_element_type=F32)
            bias, inside = bias_and_mask(j0 - ks)
            sc = jnp.where(inside, sc + bias, NEG_INF)
            mx = jnp.max(sc, axis=-1, keepdims=True)
            p = jnp.exp(sc - mx)
            den = jnp.sum(p, axis=-1, keepdims=True)
            num = _unstack_heads(_dot(p.astype(BF16), v), masks, qb)
            mx = _unstack_heads(jnp.broadcast_to(mx, (DIL_HEADS * qb, DIL_WIDTH)), masks, qb)
            den = _unstack_heads(jnp.broadcast_to(den, (DIL_HEADS * qb, DIL_WIDTH)), masks, qb)
            if gi > 0:
                m_old = _load_cols(m_refs, q_rows)
                m_new = jnp.maximum(m_old, mx)
                a_old = jnp.exp(m_old - m_new)
                a_new = jnp.exp(mx - m_new)
                mx = m_new
                den = _load_cols(l_refs, q_rows) * a_old + den * a_new
                num = _load_cols(o_refs, q_rows) * a_old + num * a_new
            _store_cols(m_refs, q_rows, mx)
            _store_cols(l_refs, q_rows, den)
            _store_cols(o_refs, q_rows, num)


def _attention_body(*refs):
    n_slab = DIL_WIDTH // LANES
    hq_refs, hk_refs, hv_refs = (refs[i * n_slab:(i + 1) * n_slab] for i in range(3))
    hm_ref, kt_ref, v_ref, out_ref = refs[3 * n_slab:3 * n_slab + 4]
    m_refs, l_refs, o_refs = (refs[3 * n_slab + 4 + i * n_slab:3 * n_slab + 4 + (i + 1) * n_slab] for i in range(3))
    g = pl.program_id(1)
    seq_len = hm_ref.shape[0]

    @pl.when(g == 0)
    def _():
        for c in range(0, seq_len, ROW_TILE):
            mem_out = _mem_attention(hm_ref[c:c + ROW_TILE, :], kt_ref[...], v_ref[...])
            out_ref[c:c + ROW_TILE, DIL_WIDTH:] = mem_out.astype(BF16)

    for gi, (_, dilation) in enumerate(DIL_GROUPS):
        @pl.when(g == gi)
        def _(gi=gi, dilation=dilation):
            _dilated_group(gi, dilation, seq_len, hq_refs, hk_refs, hv_refs, m_refs, l_refs, o_refs)

    @pl.when(g == len(DIL_GROUPS) - 1)
    def _():
        for c in range(n_slab):
            out_ref[:, c * LANES:(c + 1) * LANES] = (o_refs[c][...] / l_refs[c][...]).astype(BF16)


def _attention(h, kt, v):
    bsz, seq_len, _ = h.shape
    n_g = len(DIL_GROUPS)
    n_slab = DIL_WIDTH // LANES

    def slabs(base):
        return [pl.BlockSpec((None, seq_len, LANES), lambda i, g, c=c: (i, 0, (base + g) * n_slab + c))
                for c in range(n_slab)]

    return pl.pallas_call(
        _attention_body,
        grid=(bsz, n_g),
        in_specs=slabs(0) + slabs(n_g) + slabs(2 * n_g) + [
            pl.BlockSpec((None, seq_len, MEM_WIDTH), lambda i, g: (i, 0, 3 * n_g)),
            pl.BlockSpec((None,) + kt.shape[1:], lambda i, g: (i, 0, 0)),
            pl.BlockSpec((None,) + v.shape[1:], lambda i, g: (i, 0, 0))],
        out_specs=pl.BlockSpec((None, seq_len, DIL_WIDTH + MEM_WIDTH), lambda i, g: (i, 0, 0)),
        out_shape=jax.ShapeDtypeStruct((bsz, seq_len, DIL_WIDTH + MEM_WIDTH), BF16),
        scratch_shapes=[pltpu.VMEM((seq_len, LANES), F32)] * (3 * n_slab),
        compiler_params=_params("arbitrary", "arbitrary"),
        name="dilated_attention",
    )(*([h] * (3 * n_slab + 1)), kt, v)


def _proj_out_ln_body(a_ref, x_ref, w_ref, g_ref, b_ref, o_ref):
    y = _dot(a_ref[...], w_ref[...])
    o_ref[...] = _layer_norm(ALPHA * x_ref[...] + y, g_ref[...], b_ref[...])


def _proj_out_ln(layer, a2, x2, w, g, b):
    t, d = x2.shape
    k = a2.shape[1]
    return pl.pallas_call(
        _proj_out_ln_body,
        grid=(t // MATMUL_TILE,),
        in_specs=[pl.BlockSpec((MATMUL_TILE, k), lambda i: (i, 0)),
                  pl.BlockSpec((MATMUL_TILE, d), lambda i: (i, 0)),
                  _whole(w, layer), _whole(g), _whole(b)],
        out_specs=pl.BlockSpec((MATMUL_TILE, d), lambda i: (i, 0)),
        out_shape=jax.ShapeDtypeStruct((t, d), F32),
        compiler_params=_params("arbitrary"),
        name="proj_out_ln",
    )(a2, x2, w, g, b)


def _swiglu(xb, wg_ref, wu_ref, wd_ref, f_chunk):
    y = None
    f = wg_ref.shape[1]
    for c in range(0, f, f_chunk):
        e = min(c + f_chunk, f)
        g = _dot(xb, wg_ref[:, c:e])
        u = _dot(xb, wu_ref[:, c:e])
        part = _dot((g * jax.nn.sigmoid(g) * u).astype(BF16), wd_ref[c:e, :].astype(BF16))
        y = part if y is None else y + part
    return y


def _ffn_ln_body(x_ref, wg_ref, wu_ref, wd_ref, g_ref, b_ref, o_ref):
    x = x_ref[...]
    y = _swiglu(x.astype(BF16), wg_ref, wu_ref, wd_ref, F_CHUNK)
    o_ref[...] = _layer_norm(ALPHA * x + y, g_ref[...], b_ref[...])


def _ffn_ln(layer, x2, wg, wu, wd, g, b):
    t, d = x2.shape
    f = wg.shape[-1]
    resident = lambda w: pl.BlockSpec((None,) + w.shape[1:], lambda i: (layer, 0, 0), pipeline_mode=pl.Buffered(1))
    return pl.pallas_call(
        _ffn_ln_body,
        grid=(t // MATMUL_TILE,),
        in_specs=[pl.BlockSpec((MATMUL_TILE, d), lambda i: (i, 0)),
                  resident(wg), resident(wu), resident(wd), _whole(g), _whole(b)],
        out_specs=pl.BlockSpec((MATMUL_TILE, d), lambda i: (i, 0)),
        out_shape=jax.ShapeDtypeStruct((t, d), F32),
        compiler_params=_params("arbitrary"),
        name="ffn_ln",
    )(x2, wg, wu, wd, g, b)


_COL_E0, _COL_E1, _COL_R0, _COL_R1, _COL_W0, _COL_W1 = range(6)


def _split_bf16(a):
    hi = a.astype(BF16)
    return hi, (a - hi.astype(F32)).astype(BF16)


def _router_body(x_ref, whl_ref, tri_ref, meta_ref, route_ref, cnt_ref, carry_ref):
    i = pl.program_id(0)

    @pl.when(i == 0)
    def _():
        carry_ref[...] = jnp.zeros_like(carry_ref)

    tm = x_ref.shape[0]
    lane = lax.broadcasted_iota(jnp.int32, (tm, LANES), 1).astype(F32)
    xh, xl = _split_bf16(x_ref[...])
    both = _dot(xh, whl_ref[...])
    logits = both[:, :LANES] + (_dot(xl, whl_ref[:, :LANES]) + both[:, LANES:])
    logits = jnp.where(lane < N_EXPERTS, logits, -jnp.inf)
    m0 = jnp.max(logits, axis=-1, keepdims=True)
    e0 = jnp.min(jnp.where(logits == m0, lane, float(LANES)), axis=-1, keepdims=True)
    rest = jnp.where(lane == e0, -jnp.inf, logits)
    m1 = jnp.max(rest, axis=-1, keepdims=True)
    e1 = jnp.min(jnp.where(rest == m1, lane, float(LANES)), axis=-1, keepdims=True)
    ex = jnp.exp(m1 - m0)
    w0 = 1.0 / (1.0 + ex)
    w1 = ex / (1.0 + ex)
    hit0 = lane == e0
    hit1 = lane == e1
    onehot = (hit0 | hit1).astype(F32)
    before = _dot(tri_ref[...], onehot.astype(BF16)) + carry_ref[...]
    r0 = jnp.sum(jnp.where(hit0, before, 0.0), axis=-1, keepdims=True)
    r1 = jnp.sum(jnp.where(hit1, before, 0.0), axis=-1, keepdims=True)
    carry_ref[...] += jnp.sum(onehot, axis=0, keepdims=True)
    cnt_ref[...] = carry_ref[...]
    meta = jnp.zeros((tm, LANES), F32)
    for col, val in ((_COL_E0, e0), (_COL_E1, e1), (_COL_R0, r0), (_COL_R1, r1), (_COL_W0, w0), (_COL_W1, w1)):
        meta = jnp.where(lane == col, val, meta)
    meta_ref[...] = meta
    route_ref[...] = meta.T[:SUBLANES, :]


def _router(x2, w_router):
    t, d = x2.shape
    whl = jnp.concatenate(_split_bf16(jnp.zeros((d, LANES), F32).at[:, :N_EXPERTS].set(w_router)), axis=1)
    tri = jnp.tril(jnp.ones((ROW_TILE, ROW_TILE), BF16), -1)
    return pl.pallas_call(
        _router_body,
        grid=(t // ROW_TILE,),
        in_specs=[pl.BlockSpec((ROW_TILE, d), lambda i: (i, 0)), _whole(whl), _whole(tri)],
        out_specs=[pl.BlockSpec((ROW_TILE, LANES), lambda i: (i, 0)),
                   pl.BlockSpec((None, SUBLANES, ROW_TILE), lambda i: (i, 0, 0)),
                   pl.BlockSpec((1, LANES), lambda i: (0, 0))],
        out_shape=[jax.ShapeDtypeStruct((t, LANES), F32),
                   jax.ShapeDtypeStruct((t // ROW_TILE, SUBLANES, ROW_TILE), F32),
                   jax.ShapeDtypeStruct((1, LANES), F32)],
        scratch_shapes=[pltpu.VMEM((1, LANES), F32)],
        compiler_params=_params("arbitrary"),
        name="router",
    )(x2, whl, tri)


assert D_MODEL == SUBLANES * LANES


def _store_row_tiles(ref, val):
    n = val.shape[0]
    for c in range(SUBLANES):
        ref[pl.ds(c, n, stride=SUBLANES), :] = val[:, c * LANES:(c + 1) * LANES]


def _load_row_tiles(ref, n):
    return jnp.concatenate([ref[pl.ds(c, n, stride=SUBLANES), :] for c in range(SUBLANES)], axis=1)


def _tile_of(ref, row):
    return ref.at[pl.ds(pl.multiple_of(row * SUBLANES, SUBLANES), SUBLANES), :]


def _wait_rows(hbm_rows_ref, n, sem):
    span = hbm_rows_ref.at[pl.ds(0, n)]
    pltpu.make_async_copy(span, span, sem).wait()


def _issue_row_copies(n_tokens, copy_of):
    def group(gidx, _):
        for u in range(ISSUE_UNROLL):
            for k in range(TOP_K):
                copy_of(gidx * ISSUE_UNROLL + u, k).start(priority=k)
        return 0

    lax.fori_loop(0, n_tokens // ISSUE_UNROLL, group, 0)


def _dispatch_body(pad_end_ref, dest_ref, x_ref, xg_ref, stage_ref, zero_ref, sem, zero_sem):
    tm = x_ref.shape[0]

    @pl.when(pl.program_id(0) == 0)
    def _():
        zero_ref[...] = jnp.zeros_like(zero_ref)

        def fill(start):
            copy = pltpu.make_async_copy(zero_ref, xg_ref.at[pl.ds(start, EXPERT_BLOCK)], zero_sem)
            copy.start()
            copy.wait()

        for e in range(N_EXPERTS):
            fill(jnp.maximum(pad_end_ref[e] - EXPERT_BLOCK, 0))
            tail = pad_end_ref[N_EXPERTS - 1] + e * EXPERT_BLOCK
            pl.when(tail < xg_ref.shape[0])(functools.partial(fill, tail))

    _store_row_tiles(stage_ref, x_ref[...])
    for part in range(tm // ROW_TILE):
        _issue_row_copies(ROW_TILE, lambda tok, k, part=part: pltpu.make_async_copy(
            _tile_of(stage_ref, part * ROW_TILE + tok), xg_ref.at[dest_ref[part, 0, k * ROW_TILE + tok]], sem))
    for _ in range(TOP_K):
        _wait_rows(xg_ref, tm, sem)


def _dispatch(x2, dest, pad_end, n_rows):
    t, d = x2.shape
    tm = DISPATCH_TILE
    n_tiles = t // tm
    grid_spec = pltpu.PrefetchScalarGridSpec(
        num_scalar_prefetch=1,
        grid=(n_tiles,),
        in_specs=[pl.BlockSpec((tm // ROW_TILE, 1, TOP_K * ROW_TILE), lambda i, pe: (i, 0, 0),
                               memory_space=pltpu.SMEM),
                  pl.BlockSpec((tm, d), lambda i, pe: (i, 0))],
        out_specs=pl.BlockSpec(memory_space=pl.ANY),
        scratch_shapes=[pltpu.VMEM((tm * SUBLANES, LANES), F32),
                        pltpu.VMEM((EXPERT_BLOCK, SUBLANES, LANES), F32),
                        pltpu.SemaphoreType.DMA(()), pltpu.SemaphoreType.DMA(())],
    )
    return pl.pallas_call(
        _dispatch_body,
        grid_spec=grid_spec,
        out_shape=jax.ShapeDtypeStruct((n_rows, SUBLANES, LANES), F32),
        compiler_params=_params("arbitrary"),
        name="dispatch",
    )(pad_end, dest, x2)


def _experts_body(be_ref, nused_ref, x_ref, wg_ref, wu_ref, wd_ref, o_ref, acc_ref, xb_ref):
    blk = pl.program_id(0)
    j = pl.program_id(1)

    @pl.when((blk < nused_ref[0]) & (j == 0))
    def _():
        xb = _load_row_tiles(x_ref, EXPERT_BLOCK).astype(BF16)
        xb_ref[...] = xb
        acc_ref[...] = _swiglu(xb, wg_ref, wu_ref, wd_ref, F_CHUNK)

    @pl.when((blk < nused_ref[0]) & (j == EXPERT_F_TILES - 1))
    def _():
        _store_row_tiles(o_ref, acc_ref[...] + _swiglu(xb_ref[...], wg_ref, wu_ref, wd_ref, F_CHUNK))

    @pl.when(blk >= nused_ref[0])
    def _():
        o_ref[...] = jnp.zeros_like(o_ref)


def _experts(layer, xg, block_e, n_used, wg, wu, wd):
    assert EXPERT_F_TILES == 2
    n_rows = xg.shape[0] // SUBLANES
    d = D_MODEL
    f = wg.shape[-1]
    ft = f // EXPERT_F_TILES
    n_blocks = n_rows // EXPERT_BLOCK
    last = EXPERT_F_TILES - 1
    tile_rows = EXPERT_BLOCK * SUBLANES

    def used(blk, nu):
        return jnp.maximum(jnp.minimum(blk, nu[0] - 1), 0)

    def f_tile(blk, j, nu):
        return jnp.where(blk < nu[0], j, last)

    grid_spec = pltpu.PrefetchScalarGridSpec(
        num_scalar_prefetch=2,
        grid=(n_blocks, EXPERT_F_TILES),
        in_specs=[pl.BlockSpec((tile_rows, LANES), lambda blk, j, be, nu: (used(blk, nu), 0)),
                  pl.BlockSpec((None, None, d, ft),
                               lambda blk, j, be, nu: (layer, be[used(blk, nu)], 0, f_tile(blk, j, nu))),
                  pl.BlockSpec((None, None, d, ft),
                               lambda blk, j, be, nu: (layer, be[used(blk, nu)], 0, f_tile(blk, j, nu))),
                  pl.BlockSpec((None, None, ft, d),
                               lambda blk, j, be, nu: (layer, be[used(blk, nu)], f_tile(blk, j, nu), 0))],
        out_specs=pl.BlockSpec((tile_rows, LANES), lambda blk, j, be, nu: (blk, 0)),
        scratch_shapes=[pltpu.VMEM((EXPERT_BLOCK, d), F32), pltpu.VMEM((EXPERT_BLOCK, d), BF16)],
    )
    return pl.pallas_call(
        _experts_body,
        grid_spec=grid_spec,
        out_shape=jax.ShapeDtypeStruct(xg.shape, F32),
        compiler_params=_params("arbitrary", "arbitrary"),
        name="experts",
    )(block_e, n_used, xg, wg, wu, wd)


def _combine_ln_body(dest_ref, dest_next_ref, x_ref, w_ref, yb_ref, g_ref, b_ref, o_ref, buf_ref, sem):
    i = pl.program_id(0)
    tm = x_ref.shape[0]
    slot = lax.rem(i, 2)

    def gather(idx_ref, into):
        _issue_row_copies(tm, lambda tok, k: pltpu.make_async_copy(
            yb_ref.at[idx_ref[0, 0, k * tm + tok]], _tile_of(buf_ref.at[into, k], tok), sem.at[into]))

    @pl.when(i == 0)
    def _():
        gather(dest_ref, 0)

    @pl.when(i + 1 < pl.num_programs(0))
    def _():
        gather(dest_next_ref, 1 - slot)

    for _ in range(TOP_K):
        _wait_rows(yb_ref, tm, sem.at[slot])
    w = w_ref[...]
    y = (w[:, _COL_W0:_COL_W0 + 1] * _load_row_tiles(buf_ref.at[slot, 0], tm)
         + w[:, _COL_W1:_COL_W1 + 1] * _load_row_tiles(buf_ref.at[slot, 1], tm))
    o_ref[...] = _layer_norm(ALPHA * x_ref[...] + y, g_ref[...], b_ref[...])


def _combine_ln(x2, meta, yb, dest, g, b):
    t, d = x2.shape
    n_tiles = t // ROW_TILE
    dest_block = lambda index_map: pl.BlockSpec((1, 1, TOP_K * ROW_TILE), index_map, memory_space=pltpu.SMEM)
    return pl.pallas_call(
        _combine_ln_body,
        grid=(n_tiles,),
        in_specs=[dest_block(lambda i: (i, 0, 0)),
                  dest_block(lambda i: (jnp.minimum(i + 1, n_tiles - 1), 0, 0)),
                  pl.BlockSpec((ROW_TILE, d), lambda i: (i, 0)),
                  pl.BlockSpec((ROW_TILE, LANES), lambda i: (i, 0)),
                  pl.BlockSpec(memory_space=pl.ANY),
                  _whole(g), _whole(b)],
        out_specs=pl.BlockSpec((ROW_TILE, d), lambda i: (i, 0)),
        out_shape=jax.ShapeDtypeStruct((t, d), F32),
        scratch_shapes=[pltpu.VMEM((2, TOP_K, ROW_TILE * SUBLANES, LANES), F32), pltpu.SemaphoreType.DMA((2,))],
        compiler_params=_params("arbitrary"),
        name="combine_ln",
    )(dest, dest, x2, meta, yb, g, b)


def _moe_ln(layer, x2, w_router, wg, wu, wd, g, b):
    t, _ = x2.shape
    meta, routes, counts = _router(x2, w_router)
    counts = counts[0, :N_EXPERTS].astype(jnp.int32)
    padded = (counts + EXPERT_BLOCK - 1) // EXPERT_BLOCK * EXPERT_BLOCK
    pad_end = jnp.cumsum(padded)
    pad_start = pad_end - padded
    experts = routes[:, _COL_E0:_COL_E1 + 1, :].astype(jnp.int32)
    ranks = routes[:, _COL_R0:_COL_R1 + 1, :].astype(jnp.int32)
    dest = ranks
    for e in range(N_EXPERTS):
        dest = dest + jnp.where(experts == e, pad_start[e], 0)
    n_blocks = t * TOP_K // EXPERT_BLOCK + N_EXPERTS
    block_start = jnp.arange(n_blocks, dtype=jnp.int32) * EXPERT_BLOCK
    block_e = jnp.minimum(jnp.sum(pad_end[None, :] <= block_start[:, None], axis=1), N_EXPERTS - 1).astype(jnp.int32)
    n_used = (pad_end[-1:] // EXPERT_BLOCK).astype(jnp.int32)
    dest = dest.reshape(dest.shape[0], 1, TOP_K * ROW_TILE)
    n_rows = n_blocks * EXPERT_BLOCK
    xg = _dispatch(x2, dest, pad_end.astype(jnp.int32), n_rows)
    yb = _experts(layer, xg.reshape(n_rows * SUBLANES, LANES), block_e, n_used, wg, wu, wd)
    return _combine_ln(x2, meta, yb.reshape(n_rows, SUBLANES, LANES), dest, g, b)


def _block_diag(w):
    n_g, c, _ = w.shape
    out = jnp.zeros((n_g * c, n_g * c), w.dtype)
    for gi in range(n_g):
        out = out.at[gi * c:(gi + 1) * c, gi * c:(gi + 1) * c].set(w[gi])
    return out


def kernel(x, mem, w_mem_kv, a_w_in, a_conv_w, a_w_out, b_w_in, b_w_out, c_w_in, c_pool_w, c_pool_scale, c_w_out, ln_g, ln_b, ffn_w_gate, ffn_w_up, ffn_w_down, moe_router, moe_w_gate, moe_w_up, moe_w_down):
    bsz, seq_len, d = x.shape
    t = bsz * seq_len
    kt, v = _mem_kv(mem, w_mem_kv.astype(BF16))
    x = x.astype(F32)
    a_w_in, a_w_out, b_w_in, b_w_out, c_w_in, c_w_out = (
        w.astype(BF16) for w in (a_w_in, a_w_out, b_w_in, b_w_out, c_w_in, c_w_out))
    ffn_w = [w.astype(BF16) for w in (ffn_w_gate, ffn_w_up, ffn_w_down)]
    moe_w = [moe_w_gate.astype(BF16), moe_w_up.astype(BF16), moe_w_down]
    for i in range(DEPTH):
        kind, j = i % 3, i // 3
        g0, b0 = ln_g[i, 0].reshape(1, d), ln_b[i, 0].reshape(1, d)
        g1, b1 = ln_g[i, 1].reshape(1, d), ln_b[i, 1].reshape(1, d)
        if kind == 0:
            x = _mixer_layer(_conv_layer_body, j, x, a_w_in, [a_conv_w[j]], kt, v, a_w_out, g0, b0)
        elif kind == 1:
            h = _proj_in(j, x.reshape(t, d), b_w_in).reshape(bsz, seq_len, -1)
            a = _attention(h, kt, v)
            x = _proj_out_ln(j, a.reshape(t, -1), x.reshape(t, d), b_w_out, g0, b0)
            x = x.reshape(bsz, seq_len, d)
        else:
            x = _mixer_layer(_pool_layer_body, j, x, c_w_in,
                             [_block_diag(c_pool_w[j]).astype(BF16), c_pool_scale[j].reshape(1, -1)], kt, v,
                             c_w_out, g0, b0)
        f = i // 2
        x2 = x.reshape(t, d)
        if i % 2 == 0:
            x2 = _ffn_ln(f, x2, *ffn_w, g1, b1)
        else:
            x2 = _moe_ln(f, x2, moe_router[f], *moe_w, g1, b1)
        x = x2.reshape(bsz, seq_len, d)
    return x
```

```python
import functools

import jax
import jax.numpy as jnp
from jax import lax
from jax.experimental import pallas as pl
from jax.experimental.pallas import tpu as pltpu

F32 = jnp.float32
BF16 = jnp.bfloat16

D_MODEL = 1024
DEPTH = 4
HEAD_DIM = 64
MEM_HEADS = 4
MEM_WIDTH = MEM_HEADS * HEAD_DIM
MIX_WIDTH = 3 * D_MODEL // 4
DIL_GROUPS = ((128, 1), (512, 4), (2048, 16))
DIL_HEADS = 4
DIL_WIDTH = DIL_HEADS * HEAD_DIM
DIL_RADIUS = 64
POOL_WINDOWS = (2, 4, 8, 16)
POOL_GROUP = MIX_WIDTH // len(POOL_WINDOWS)
N_EXPERTS = 8
TOP_K = 2
LN_EPS = 1e-5
NEG_INF = -1e30
ALPHA = (2 * DEPTH) ** 0.25

LANES = 128
SUBLANES = 8
HALO = SUBLANES
SEQ_TILE = 1024
SUB_TILE = 512
ROW_TILE = 512
MATMUL_TILE = 1024
F_CHUNK = 768
DISPATCH_TILE = ROW_TILE
EXPERT_BLOCK = 512
EXPERT_F_TILES = 2
Q_BLOCK = 128
ISSUE_UNROLL = 8
VMEM_LIMIT = 56 * 1024 * 1024


def _params(*sem):
    return pltpu.CompilerParams(dimension_semantics=sem, vmem_limit_bytes=VMEM_LIMIT)


def _layer_norm(v, g, b):
    mu = jnp.mean(v, axis=-1, keepdims=True)
    d = v - mu
    var = jnp.mean(d * d, axis=-1, keepdims=True)
    return d * lax.rsqrt(var + LN_EPS) * g + b


def _dot(a, b):
    return jnp.dot(a, b, preferred_element_type=F32)


def _head_masks(width):
    col = lax.broadcasted_iota(jnp.int32, (1, width), 1)
    return [(col >= h * HEAD_DIM) & (col < (h + 1) * HEAD_DIM) for h in range(width // HEAD_DIM)]


def _stack_heads(q, masks):
    q = q * HEAD_DIM ** -0.5
    return jnp.concatenate([jnp.where(m, q, 0.0) for m in masks], axis=0).astype(BF16)


def _unstack_heads(o, masks, n):
    out = o[(len(masks) - 1) * n:]
    for h in range(len(masks) - 2, -1, -1):
        out = jnp.where(masks[h], o[h * n:(h + 1) * n], out)
    return out


def _mem_attention(q, kt, v):
    n = q.shape[0]
    masks = _head_masks(MEM_WIDTH)
    sc = _dot(_stack_heads(q, masks), kt)
    p = jnp.exp(sc - jnp.max(sc, axis=-1, keepdims=True))
    p = p / jnp.sum(p, axis=-1, keepdims=True)
    return _unstack_heads(_dot(p.astype(BF16), v), masks, n)


def _mem_kv_body(mem_ref, w_ref, kt_ref, v_ref):
    kv = _dot(mem_ref[...].astype(BF16), w_ref[...])
    kt_ref[...] = kv[:, :MEM_WIDTH].T.astype(BF16)
    v_ref[...] = kv[:, MEM_WIDTH:].astype(BF16)


def _mem_kv(mem, w_kv):
    b, m, d = mem.shape
    return pl.pallas_call(
        _mem_kv_body,
        grid=(b,),
        in_specs=[pl.BlockSpec((None, m, d), lambda i: (i, 0, 0)),
                  pl.BlockSpec((d, 2 * MEM_WIDTH), lambda i: (0, 0))],
        out_specs=[pl.BlockSpec((None, MEM_WIDTH, m), lambda i: (i, 0, 0)),
                   pl.BlockSpec((None, m, MEM_WIDTH), lambda i: (i, 0, 0))],
        out_shape=[jax.ShapeDtypeStruct((b, MEM_WIDTH, m), BF16),
                   jax.ShapeDtypeStruct((b, m, MEM_WIDTH), BF16)],
        compiler_params=_params("arbitrary"),
        name="mem_kv",
    )(mem, w_kv)


def _sub_tiles(xp_ref, xm_ref, xn_ref, seq_len):
    ts = xm_ref.shape[0]
    first = pl.program_id(1) * ts
    xm = xm_ref[...]
    xe = jnp.concatenate([xp_ref[...], xm, xn_ref[...]], axis=0).astype(BF16)
    pos = first - HALO + lax.broadcasted_iota(jnp.int32, (ts + 2 * HALO, 1), 0)
    valid = (pos >= 0) & (pos < seq_len)
    for r0 in range(0, ts, SUB_TILE):
        ext = slice(r0, r0 + SUB_TILE + 2 * HALO)
        yield r0, first + r0, xm[r0:r0 + SUB_TILE], xe[ext], valid[ext]


def _shift_rows(a, k):
    n = a.shape[0]
    return pltpu.roll(a, k % n, 0)


def _mixer_tail(mix, q_mem, xm, kt_ref, v_ref, wout_ref, g_ref, b_ref, o_ref, r0):
    mem_out = _mem_attention(q_mem, kt_ref[...], v_ref[...])
    y = (_dot(mix.astype(BF16), wout_ref[:MIX_WIDTH, :])
         + _dot(mem_out.astype(BF16), wout_ref[MIX_WIDTH:, :]))
    o_ref[r0:r0 + SUB_TILE, :] = _layer_norm(ALPHA * xm + y, g_ref[...], b_ref[...])


def _conv_layer_body(seq_len, xp_ref, xm_ref, xn_ref, win_ref, cw_ref, kt_ref, v_ref,
                     wout_ref, g_ref, b_ref, o_ref):
    n = SUB_TILE
    cw = cw_ref[...]
    for r0, _, xm, xe, valid in _sub_tiles(xp_ref, xm_ref, xn_ref, seq_len):
        xmb = xm.astype(BF16)
        gate_b = _dot(xmb, win_ref[:, :MIX_WIDTH])
        cu = _dot(xe, win_ref[:, MIX_WIDTH:3 * MIX_WIDTH])
        q_mem = _dot(xmb, win_ref[:, 3 * MIX_WIDTH:])
        z = jnp.where(valid, cu[:, :MIX_WIDTH] * cu[:, MIX_WIDTH:], 0.0)
        conv = (cw[0:1] * _shift_rows(z, 1)[HALO:HALO + n]
                + cw[1:2] * z[HALO:HALO + n]
                + cw[2:3] * _shift_rows(z, -1)[HALO:HALO + n])
        _mixer_tail(gate_b * conv, q_mem, xm, kt_ref, v_ref, wout_ref, g_ref, b_ref, o_ref, r0)


def _pool_layer_body(seq_len, xp_ref, xm_ref, xn_ref, win_ref, pw_ref, ps_ref, kt_ref, v_ref,
                     wout_ref, g_ref, b_ref, o_ref):
    n = SUB_TILE
    col = lax.broadcasted_iota(jnp.int32, (1, MIX_WIDTH), 1)
    for r0, first, xm, xe, valid in _sub_tiles(xp_ref, xm_ref, xn_ref, seq_len):
        u = jnp.where(valid, _dot(xe, win_ref[:, :MIX_WIDTH]), 0.0)
        q_mem = _dot(xm.astype(BF16), win_ref[:, MIX_WIDTH:])
        a2 = u + _shift_rows(u, 1)
        a4 = _shift_rows(a2, 1) + _shift_rows(a2, -1)
        a8 = _shift_rows(a4, 2) + _shift_rows(a4, -2)
        a16 = _shift_rows(a8, 4) + _shift_rows(a8, -4)
        pos = first + lax.broadcasted_iota(jnp.int32, (n, 1), 0)
        num = a16[HALO:HALO + n]
        cnt = None
        for gi in range(len(POOL_WINDOWS) - 1, -1, -1):
            w = POOL_WINDOWS[gi]
            c_w = (jnp.minimum(pos + (w // 2 - 1), seq_len - 1) - jnp.maximum(pos - w // 2, 0) + 1).astype(F32)
            if cnt is None:
                cnt = jnp.broadcast_to(c_w, (n, MIX_WIDTH))
            else:
                in_group = col < (gi + 1) * POOL_GROUP
                num = jnp.where(in_group, (a2, a4, a8)[gi][HALO:HALO + n], num)
                cnt = jnp.where(in_group, c_w, cnt)
        diff = num / cnt - u[HALO:HALO + n]
        mix = _dot(diff.astype(BF16), pw_ref[...]) * ps_ref[...]
        _mixer_tail(mix, q_mem, xm, kt_ref, v_ref, wout_ref, g_ref, b_ref, o_ref, r0)


def _whole(w, layer=None):
    if layer is None:
        return pl.BlockSpec(w.shape, lambda *_: (0,) * w.ndim)
    return pl.BlockSpec((None,) + w.shape[1:], lambda *_: (layer,) + (0,) * (w.ndim - 1))


def _mixer_layer(body, layer, x, w_in, extra, kt, v, w_out, g, b):
    bsz, seq_len, d = x.shape
    ts = SEQ_TILE
    n_halo_blocks = seq_len // HALO
    in_specs = [
        pl.BlockSpec((None, HALO, d), lambda i, s: (i, jnp.maximum(s * (ts // HALO) - 1, 0), 0)),
        pl.BlockSpec((None, ts, d), lambda i, s: (i, s, 0)),
        pl.BlockSpec((None, HALO, d), lambda i, s: (i, jnp.minimum((s + 1) * (ts // HALO), n_halo_blocks - 1), 0)),
        _whole(w_in, layer),
    ]
    in_specs += [_whole(e) for e in extra]
    in_specs += [
        pl.BlockSpec((None,) + kt.shape[1:], lambda i, s: (i, 0, 0)),
        pl.BlockSpec((None,) + v.shape[1:], lambda i, s: (i, 0, 0)),
        _whole(w_out, layer),
        _whole(g),
        _whole(b),
    ]
    return pl.pallas_call(
        functools.partial(body, seq_len),
        grid=(bsz, seq_len // ts),
        in_specs=in_specs,
        out_specs=pl.BlockSpec((None, ts, d), lambda i, s: (i, s, 0)),
        out_shape=jax.ShapeDtypeStruct(x.shape, F32),
        compiler_params=_params("arbitrary", "arbitrary"),
        name=body.__name__.strip("_"),
    )(x, x, x, w_in, *extra, kt, v, w_out, g, b)


def _proj_in_body(x_ref, w_ref, o_ref):
    o_ref[...] = _dot(x_ref[...].astype(BF16), w_ref[...])


def _proj_in(layer, x2, w):
    t, d = x2.shape
    n = w.shape[-1]
    return pl.pallas_call(
        _proj_in_body,
        grid=(t // MATMUL_TILE,),
        in_specs=[pl.BlockSpec((MATMUL_TILE, d), lambda i: (i, 0)),
                  _whole(w, layer)],
        out_specs=pl.BlockSpec((MATMUL_TILE, n), lambda i: (i, 0)),
        out_shape=jax.ShapeDtypeStruct((t, n), F32),
        compiler_params=_params("arbitrary"),
        name="proj_in",
    )(x2, w)


def _alibi_slope(index, total):
    return 2.0 ** (-8.0 * (index + 1) / total)


def _rows(start, size, stride):
    return pl.ds(start, size) if stride == 1 else pl.ds(start, size, stride=stride)


def _load_cols(refs, rows):
    return jnp.concatenate([r[rows, :] for r in refs], axis=1)


def _store_cols(refs, rows, val):
    for c, r in enumerate(refs):
        r[rows, :] = val[:, c * LANES:(c + 1) * LANES]


def _dilated_group(gi, dilation, seq_len, hq_refs, hk_refs, hv_refs, m_refs, l_refs, o_refs):
    n_sub = seq_len // dilation
    qb = min(Q_BLOCK, n_sub)
    kw = min(qb + 2 * DIL_RADIUS, n_sub)
    masks = _head_masks(DIL_WIDTH)
    n_heads_total = len(DIL_GROUPS) * DIL_HEADS
    bias_cache = {}

    def bias_and_mask(offset):
        if offset not in bias_cache:
            rel = offset + lax.broadcasted_iota(jnp.int32, (qb, kw), 0) - lax.broadcasted_iota(jnp.int32, (qb, kw), 1)
            dist = jnp.abs(rel)
            inside = dist <= DIL_RADIUS
            span = (dist * dilation).astype(F32)
            bias = jnp.concatenate(
                [-_alibi_slope(gi * DIL_HEADS + h, n_heads_total) * span for h in range(DIL_HEADS)], axis=0)
            bias_cache[offset] = (bias, jnp.concatenate([inside] * DIL_HEADS, axis=0))
        return bias_cache[offset]

    for r in range(dilation):
        for j0 in range(0, n_sub, qb):
            ks = min(max(j0 - DIL_RADIUS, 0), n_sub - kw)
            q_rows = _rows(r + j0 * dilation, qb, dilation)
            k_rows = _rows(r + ks * dilation, kw, dilation)
            q = _load_cols(hq_refs, q_rows)
            k = _load_cols(hk_refs, k_rows).astype(BF16)
            v = _load_cols(hv_refs, k_rows).astype(BF16)
            sc = lax.dot_general(_stack_heads(q, masks), k, (((1,), (1,)), ((), ())),
                                 preferred_element_type=F32)
            bias, inside = bias_and_mask(j0 - ks)
            sc = jnp.where(inside, sc + bias, NEG_INF)
            mx = jnp.max(sc, axis=-1, keepdims=True)
            p = jnp.exp(sc - mx)
            den = jnp.sum(p, axis=-1, keepdims=True)
            num = _unstack_heads(_dot(p.astype(BF16), v), masks, qb)
            mx = _unstack_heads(jnp.broadcast_to(mx, (DIL_HEADS * qb, DIL_WIDTH)), masks, qb)
            den = _unstack_heads(jnp.broadcast_to(den, (DIL_HEADS * qb, DIL_WIDTH)), masks, qb)
            if gi > 0:
                m_old = _load_cols(m_refs, q_rows)
                m_new = jnp.maximum(m_old, mx)
                a_old = jnp.exp(m_old - m_new)
                a_new = jnp.exp(mx - m_new)
                mx = m_new
                den = _load_cols(l_refs, q_rows) * a_old + den * a_new
                num = _load_cols(o_refs, q_rows) * a_old + num * a_new
            _store_cols(m_refs, q_rows, mx)
            _store_cols(l_refs, q_rows, den)
            _store_cols(o_refs, q_rows, num)


def _attention_body(*refs):
    n_slab = DIL_WIDTH // LANES
    hq_refs, hk_refs, hv_refs = (refs[i * n_slab:(i + 1) * n_slab] for i in range(3))
    hm_ref, kt_ref, v_ref, out_ref = refs[3 * n_slab:3 * n_slab + 4]
    m_refs, l_refs, o_refs = (refs[3 * n_slab + 4 + i * n_slab:3 * n_slab + 4 + (i + 1) * n_slab] for i in range(3))
    g = pl.program_id(1)
    seq_len = hm_ref.shape[0]

    @pl.when(g == 0)
    def _():
        for c in range(0, seq_len, ROW_TILE):
            mem_out = _mem_attention(hm_ref[c:c + ROW_TILE, :], kt_ref[...], v_ref[...])
            out_ref[c:c + ROW_TILE, DIL_WIDTH:] = mem_out.astype(BF16)

    for gi, (_, dilation) in enumerate(DIL_GROUPS):
        @pl.when(g == gi)
        def _(gi=gi, dilation=dilation):
            _dilated_group(gi, dilation, seq_len, hq_refs, hk_refs, hv_refs, m_refs, l_refs, o_refs)

    @pl.when(g == len(DIL_GROUPS) - 1)
    def _():
        for c in range(n_slab):
            out_ref[:, c * LANES:(c + 1) * LANES] = (o_refs[c][...] / l_refs[c][...]).astype(BF16)


def _attention(h, kt, v):
    bsz, seq_len, _ = h.shape
    n_g = len(DIL_GROUPS)
    n_slab = DIL_WIDTH // LANES

    def slabs(base):
        return [pl.BlockSpec((None, seq_len, LANES), lambda i, g, c=c: (i, 0, (base + g) * n_slab + c))
                for c in range(n_slab)]

    return pl.pallas_call(
        _attention_body,
        grid=(bsz, n_g),
        in_specs=slabs(0) + slabs(n_g) + slabs(2 * n_g) + [
            pl.BlockSpec((None, seq_len, MEM_WIDTH), lambda i, g: (i, 0, 3 * n_g)),
            pl.BlockSpec((None,) + kt.shape[1:], lambda i, g: (i, 0, 0)),
            pl.BlockSpec((None,) + v.shape[1:], lambda i, g: (i, 0, 0))],
        out_specs=pl.BlockSpec((None, seq_len, DIL_WIDTH + MEM_WIDTH), lambda i, g: (i, 0, 0)),
        out_shape=jax.ShapeDtypeStruct((bsz, seq_len, DIL_WIDTH + MEM_WIDTH), BF16),
        scratch_shapes=[pltpu.VMEM((seq_len, LANES), F32)] * (3 * n_slab),
        compiler_params=_params("arbitrary", "arbitrary"),
        name="dilated_attention",
    )(*([h] * (3 * n_slab + 1)), kt, v)


def _proj_out_ln_body(a_ref, x_ref, w_ref, g_ref, b_ref, o_ref):
    y = _dot(a_ref[...], w_ref[...])
    o_ref[...] = _layer_norm(ALPHA * x_ref[...] + y, g_ref[...], b_ref[...])


def _proj_out_ln(layer, a2, x2, w, g, b):
    t, d = x2.shape
    k = a2.shape[1]
    return pl.pallas_call(
        _proj_out_ln_body,
        grid=(t // MATMUL_TILE,),
        in_specs=[pl.BlockSpec((MATMUL_TILE, k), lambda i: (i, 0)),
                  pl.BlockSpec((MATMUL_TILE, d), lambda i: (i, 0)),
                  _whole(w, layer), _whole(g), _whole(b)],
        out_specs=pl.BlockSpec((MATMUL_TILE, d), lambda i: (i, 0)),
        out_shape=jax.ShapeDtypeStruct((t, d), F32),
        compiler_params=_params("arbitrary"),
        name="proj_out_ln",
    )(a2, x2, w, g, b)


def _swiglu(xb, wg_ref, wu_ref, wd_ref, f_chunk):
    y = None
    f = wg_ref.shape[1]
    for c in range(0, f, f_chunk):
        e = min(c + f_chunk, f)
        g = _dot(xb, wg_ref[:, c:e])
        u = _dot(xb, wu_ref[:, c:e])
        part = _dot((g * jax.nn.sigmoid(g) * u).astype(BF16), wd_ref[c:e, :].astype(BF16))
        y = part if y is None else y + part
    return y


def _ffn_ln_body(x_ref, wg_ref, wu_ref, wd_ref, g_ref, b_ref, o_ref):
    x = x_ref[...]
    y = _swiglu(x.astype(BF16), wg_ref, wu_ref, wd_ref, F_CHUNK)
    o_ref[...] = _layer_norm(ALPHA * x + y, g_ref[...], b_ref[...])


def _ffn_ln(layer, x2, wg, wu, wd, g, b):
    t, d = x2.shape
    f = wg.shape[-1]
    resident = lambda w: pl.BlockSpec((None,) + w.shape[1:], lambda i: (layer, 0, 0), pipeline_mode=pl.Buffered(1))
    return pl.pallas_call(
        _ffn_ln_body,
        grid=(t // MATMUL_TILE,),
        in_specs=[pl.BlockSpec((MATMUL_TILE, d), lambda i: (i, 0)),
                  resident(wg), resident(wu), resident(wd), _whole(g), _whole(b)],
        out_specs=pl.BlockSpec((MATMUL_TILE, d), lambda i: (i, 0)),
        out_shape=jax.ShapeDtypeStruct((t, d), F32),
        compiler_params=_params("arbitrary"),
        name="ffn_ln",
    )(x2, wg, wu, wd, g, b)


_COL_E0, _COL_E1, _COL_R0, _COL_R1, _COL_W0, _COL_W1 = range(6)


def _split_bf16(a):
    hi = a.astype(BF16)
    return hi, (a - hi.astype(F32)).astype(BF16)


def _router_body(x_ref, whl_ref, tri_ref, meta_ref, route_ref, cnt_ref, carry_ref):
    i = pl.program_id(0)

    @pl.when(i == 0)
    def _():
        carry_ref[...] = jnp.zeros_like(carry_ref)

    tm = x_ref.shape[0]
    lane = lax.broadcasted_iota(jnp.int32, (tm, LANES), 1).astype(F32)
    xh, xl = _split_bf16(x_ref[...])
    both = _dot(xh, whl_ref[...])
    logits = both[:, :LANES] + (_dot(xl, whl_ref[:, :LANES]) + both[:, LANES:])
    logits = jnp.where(lane < N_EXPERTS, logits, -jnp.inf)
    m0 = jnp.max(logits, axis=-1, keepdims=True)
    e0 = jnp.min(jnp.where(logits == m0, lane, float(LANES)), axis=-1, keepdims=True)
    rest = jnp.where(lane == e0, -jnp.inf, logits)
    m1 = jnp.max(rest, axis=-1, keepdims=True)
    e1 = jnp.min(jnp.where(rest == m1, lane, float(LANES)), axis=-1, keepdims=True)
    ex = jnp.exp(m1 - m0)
    w0 = 1.0 / (1.0 + ex)
    w1 = ex / (1.0 + ex)
    hit0 = lane == e0
    hit1 = lane == e1
    onehot = (hit0 | hit1).astype(F32)
    before = _dot(tri_ref[...], onehot.astype(BF16)) + carry_ref[...]
    r0 = jnp.sum(jnp.where(hit0, before, 0.0), axis=-1, keepdims=True)
    r1 = jnp.sum(jnp.where(hit1, before, 0.0), axis=-1, keepdims=True)
    carry_ref[...] += jnp.sum(onehot, axis=0, keepdims=True)
    cnt_ref[...] = carry_ref[...]
    meta = jnp.zeros((tm, LANES), F32)
    for col, val in ((_COL_E0, e0), (_COL_E1, e1), (_COL_R0, r0), (_COL_R1, r1), (_COL_W0, w0), (_COL_W1, w1)):
        meta = jnp.where(lane == col, val, meta)
    meta_ref[...] = meta
    route_ref[...] = meta.T[:SUBLANES, :]


def _router(x2, w_router):
    t, d = x2.shape
    whl = jnp.concatenate(_split_bf16(jnp.zeros((d, LANES), F32).at[:, :N_EXPERTS].set(w_router)), axis=1)
    tri = jnp.tril(jnp.ones((ROW_TILE, ROW_TILE), BF16), -1)
    return pl.pallas_call(
        _router_body,
        grid=(t // ROW_TILE,),
        in_specs=[pl.BlockSpec((ROW_TILE, d), lambda i: (i, 0)), _whole(whl), _whole(tri)],
        out_specs=[pl.BlockSpec((ROW_TILE, LANES), lambda i: (i, 0)),
                   pl.BlockSpec((None, SUBLANES, ROW_TILE), lambda i: (i, 0, 0)),
                   pl.BlockSpec((1, LANES), lambda i: (0, 0))],
        out_shape=[jax.ShapeDtypeStruct((t, LANES), F32),
                   jax.ShapeDtypeStruct((t // ROW_TILE, SUBLANES, ROW_TILE), F32),
                   jax.ShapeDtypeStruct((1, LANES), F32)],
        scratch_shapes=[pltpu.VMEM((1, LANES), F32)],
        compiler_params=_params("arbitrary"),
        name="router",
    )(x2, whl, tri)


assert D_MODEL == SUBLANES * LANES


def _store_row_tiles(ref, val):
    n = val.shape[0]
    for c in range(SUBLANES):
        ref[pl.ds(c, n, stride=SUBLANES), :] = val[:, c * LANES:(c + 1) * LANES]


def _load_row_tiles(ref, n):
    return jnp.concatenate([ref[pl.ds(c, n, stride=SUBLANES), :] for c in range(SUBLANES)], axis=1)


def _tile_of(ref, row):
    return ref.at[pl.ds(pl.multiple_of(row * SUBLANES, SUBLANES), SUBLANES), :]


def _wait_rows(hbm_rows_ref, n, sem):
    span = hbm_rows_ref.at[pl.ds(0, n)]
    pltpu.make_async_copy(span, span, sem).wait()


def _issue_row_copies(n_tokens, copy_of):
    def group(gidx, _):
        for u in range(ISSUE_UNROLL):
            for k in range(TOP_K):
                copy_of(gidx * ISSUE_UNROLL + u, k).start(priority=k)
        return 0

    lax.fori_loop(0, n_tokens // ISSUE_UNROLL, group, 0)


def _dispatch_body(pad_end_ref, dest_ref, x_ref, xg_ref, stage_ref, zero_ref, sem, zero_sem):
    tm = x_ref.shape[0]

    @pl.when(pl.program_id(0) == 0)
    def _():
        zero_ref[...] = jnp.zeros_like(zero_ref)
        fills = []
        for e in range(N_EXPERTS):
            seg_start = pad_end_ref[e - 1] if e else 0
            fills.append((pad_end_ref[e] > seg_start, pad_end_ref[e] - EXPERT_BLOCK))
            tail = pad_end_ref[N_EXPERTS - 1] + e * EXPERT_BLOCK
            fills.append((tail < xg_ref.shape[0], tail))

        def fill(start):
            return pltpu.make_async_copy(zero_ref, xg_ref.at[pl.ds(start, EXPERT_BLOCK)], zero_sem)

        for cond, start in fills:
            pl.when(cond)(lambda start=start: fill(start).start())
        for cond, start in fills:
            pl.when(cond)(lambda start=start: fill(start).wait())

    _store_row_tiles(stage_ref, x_ref[...])
    for part in range(tm // ROW_TILE):
        _issue_row_copies(ROW_TILE, lambda tok, k, part=part: pltpu.make_async_copy(
            _tile_of(stage_ref, part * ROW_TILE + tok), xg_ref.at[dest_ref[part, 0, k * ROW_TILE + tok]], sem))
    for _ in range(TOP_K):
        _wait_rows(xg_ref, tm, sem)


def _dispatch(x2, dest, pad_end, n_rows):
    t, d = x2.shape
    tm = DISPATCH_TILE
    n_tiles = t // tm
    grid_spec = pltpu.PrefetchScalarGridSpec(
        num_scalar_prefetch=1,
        grid=(n_tiles,),
        in_specs=[pl.BlockSpec((tm // ROW_TILE, 1, TOP_K * ROW_TILE), lambda i, pe: (i, 0, 0),
                               memory_space=pltpu.SMEM),
                  pl.BlockSpec((tm, d), lambda i, pe: (i, 0))],
        out_specs=pl.BlockSpec(memory_space=pl.ANY),
        scratch_shapes=[pltpu.VMEM((tm * SUBLANES, LANES), F32),
                        pltpu.VMEM((EXPERT_BLOCK, SUBLANES, LANES), F32),
                        pltpu.SemaphoreType.DMA(()), pltpu.SemaphoreType.DMA(())],
    )
    return pl.pallas_call(
        _dispatch_body,
        grid_spec=grid_spec,
        out_shape=jax.ShapeDtypeStruct((n_rows, SUBLANES, LANES), F32),
        compiler_params=_params("arbitrary"),
        name="dispatch",
    )(pad_end, dest, x2)


def _experts_body(be_ref, nused_ref, x_ref, wg_ref, wu_ref, wd_ref, o_ref, acc_ref, xb_ref):
    blk = pl.program_id(0)
    j = pl.program_id(1)

    @pl.when((blk < nused_ref[0]) & (j == 0))
    def _():
        xb = _load_row_tiles(x_ref, EXPERT_BLOCK).astype(BF16)
        xb_ref[...] = xb
        acc_ref[...] = _swiglu(xb, wg_ref, wu_ref, wd_ref, F_CHUNK)

    @pl.when((blk < nused_ref[0]) & (j == EXPERT_F_TILES - 1))
    def _():
        _store_row_tiles(o_ref, acc_ref[...] + _swiglu(xb_ref[...], wg_ref, wu_ref, wd_ref, F_CHUNK))

    @pl.when(blk >= nused_ref[0])
    def _():
        o_ref[...] = jnp.zeros_like(o_ref)


def _experts(layer, xg, block_e, n_used, wg, wu, wd):
    assert EXPERT_F_TILES == 2
    n_rows = xg.shape[0] // SUBLANES
    d = D_MODEL
    f = wg.shape[-1]
    ft = f // EXPERT_F_TILES
    n_blocks = n_rows // EXPERT_BLOCK
    last = EXPERT_F_TILES - 1
    tile_rows = EXPERT_BLOCK * SUBLANES

    def used(blk, nu):
        return jnp.maximum(jnp.minimum(blk, nu[0] - 1), 0)

    def f_tile(blk, j, nu):
        return jnp.where(blk < nu[0], j, last)

    grid_spec = pltpu.PrefetchScalarGridSpec(
        num_scalar_prefetch=2,
        grid=(n_blocks, EXPERT_F_TILES),
        in_specs=[pl.BlockSpec((tile_rows, LANES), lambda blk, j, be, nu: (used(blk, nu), 0)),
                  pl.BlockSpec((None, None, d, ft),
                               lambda blk, j, be, nu: (layer, be[used(blk, nu)], 0, f_tile(blk, j, nu))),
                  pl.BlockSpec((None, None, d, ft),
                               lambda blk, j, be, nu: (layer, be[used(blk, nu)], 0, f_tile(blk, j, nu))),
                  pl.BlockSpec((None, None, ft, d),
                               lambda blk, j, be, nu: (layer, be[used(blk, nu)], f_tile(blk, j, nu), 0))],
        out_specs=pl.BlockSpec((tile_rows, LANES), lambda blk, j, be, nu: (blk, 0)),
        scratch_shapes=[pltpu.VMEM((EXPERT_BLOCK, d), F32), pltpu.VMEM((EXPERT_BLOCK, d), BF16)],
    )
    return pl.pallas_call(
        _experts_body,
        grid_spec=grid_spec,
        out_shape=jax.ShapeDtypeStruct(xg.shape, F32),
        compiler_params=_params("arbitrary", "arbitrary"),
        name="experts",
    )(block_e, n_used, xg, wg, wu, wd)


def _combine_ln_body(dest_ref, dest_next_ref, x_ref, w_ref, yb_ref, g_ref, b_ref, o_ref, buf_ref, sem):
    i = pl.program_id(0)
    tm = x_ref.shape[0]
    slot = lax.rem(i, 2)

    def gather(idx_ref, into):
        _issue_row_copies(tm, lambda tok, k: pltpu.make_async_copy(
            yb_ref.at[idx_ref[0, 0, k * tm + tok]], _tile_of(buf_ref.at[into, k], tok), sem.at[into]))

    @pl.when(i == 0)
    def _():
        gather(dest_ref, 0)

    @pl.when(i + 1 < pl.num_programs(0))
    def _():
        gather(dest_next_ref, 1 - slot)

    for _ in range(TOP_K):
        _wait_rows(yb_ref, tm, sem.at[slot])
    w = w_ref[...]
    y = (w[:, _COL_W0:_COL_W0 + 1] * _load_row_tiles(buf_ref.at[slot, 0], tm)
         + w[:, _COL_W1:_COL_W1 + 1] * _load_row_tiles(buf_ref.at[slot, 1], tm))
    o_ref[...] = _layer_norm(ALPHA * x_ref[...] + y, g_ref[...], b_ref[...])


def _combine_ln(x2, meta, yb, dest, g, b):
    t, d = x2.shape
    n_tiles = t // ROW_TILE
    dest_block = lambda index_map: pl.BlockSpec((1, 1, TOP_K * ROW_TILE), index_map, memory_space=pltpu.SMEM)
    return pl.pallas_call(
        _combine_ln_body,
        grid=(n_tiles,),
        in_specs=[dest_block(lambda i: (i, 0, 0)),
                  dest_block(lambda i: (jnp.minimum(i + 1, n_tiles - 1), 0, 0)),
                  pl.BlockSpec((ROW_TILE, d), lambda i: (i, 0)),
                  pl.BlockSpec((ROW_TILE, LANES), lambda i: (i, 0)),
                  pl.BlockSpec(memory_space=pl.ANY),
                  _whole(g), _whole(b)],
        out_specs=pl.BlockSpec((ROW_TILE, d), lambda i: (i, 0)),
        out_shape=jax.ShapeDtypeStruct((t, d), F32),
        scratch_shapes=[pltpu.VMEM((2, TOP_K, ROW_TILE * SUBLANES, LANES), F32), pltpu.SemaphoreType.DMA((2,))],
        compiler_params=_params("arbitrary"),
        name="combine_ln",
    )(dest, dest, x2, meta, yb, g, b)


def _moe_ln(layer, x2, w_router, wg, wu, wd, g, b):
    t, _ = x2.shape
    meta, routes, counts = _router(x2, w_router)
    counts = counts[0, :N_EXPERTS].astype(jnp.int32)
    padded = (counts + EXPERT_BLOCK - 1) // EXPERT_BLOCK * EXPERT_BLOCK
    pad_end = jnp.cumsum(padded)
    pad_start = pad_end - padded
    experts = routes[:, _COL_E0:_COL_E1 + 1, :].astype(jnp.int32)
    ranks = routes[:, _COL_R0:_COL_R1 + 1, :].astype(jnp.int32)
    dest = ranks
    for e in range(N_EXPERTS):
        dest = dest + jnp.where(experts == e, pad_start[e], 0)
    n_blocks = t * TOP_K // EXPERT_BLOCK + N_EXPERTS
    block_start = jnp.arange(n_blocks, dtype=jnp.int32) * EXPERT_BLOCK
    block_e = jnp.minimum(jnp.sum(pad_end[None, :] <= block_start[:, None], axis=1), N_EXPERTS - 1).astype(jnp.int32)
    n_used = (pad_end[-1:] // EXPERT_BLOCK).astype(jnp.int32)
    dest = dest.reshape(dest.shape[0], 1, TOP_K * ROW_TILE)
    n_rows = n_blocks * EXPERT_BLOCK
    xg = _dispatch(x2, dest, pad_end.astype(jnp.int32), n_rows)
    yb = _experts(layer, xg.reshape(n_rows * SUBLANES, LANES), block_e, n_used, wg, wu, wd)
    return _combine_ln(x2, meta, yb.reshape(n_rows, SUBLANES, LANES), dest, g, b)


def _block_diag(w):
    n_g, c, _ = w.shape
    out = jnp.zeros((n_g * c, n_g * c), w.dtype)
    for gi in range(n_g):
        out = out.at[gi * c:(gi + 1) * c, gi * c:(gi + 1) * c].set(w[gi])
    return out


def kernel(x, mem, w_mem_kv, a_w_in, a_conv_w, a_w_out, b_w_in, b_w_out, c_w_in, c_pool_w, c_pool_scale, c_w_out, ln_g, ln_b, ffn_w_gate, ffn_w_up, ffn_w_down, moe_router, moe_w_gate, moe_w_up, moe_w_down):
    bsz, seq_len, d = x.shape
    t = bsz * seq_len
    kt, v = _mem_kv(mem, w_mem_kv.astype(BF16))
    x = x.astype(F32)
    a_w_in, a_w_out, b_w_in, b_w_out, c_w_in, c_w_out = (
        w.astype(BF16) for w in (a_w_in, a_w_out, b_w_in, b_w_out, c_w_in, c_w_out))
    ffn_w = [w.astype(BF16) for w in (ffn_w_gate, ffn_w_up, ffn_w_down)]
    moe_w = [moe_w_gate.astype(BF16), moe_w_up.astype(BF16), moe_w_down]
    for i in range(DEPTH):
        kind, j = i % 3, i // 3
        g0, b0 = ln_g[i, 0].reshape(1, d), ln_b[i, 0].reshape(1, d)
        g1, b1 = ln_g[i, 1].reshape(1, d), ln_b[i, 1].reshape(1, d)
        if kind == 0:
            x = _mixer_layer(_conv_layer_body, j, x, a_w_in, [a_conv_w[j]], kt, v, a_w_out, g0, b0)
        elif kind == 1:
            h = _proj_in(j, x.reshape(t, d), b_w_in).reshape(bsz, seq_len, -1)
            a = _attention(h, kt, v)
            x = _proj_out_ln(j, a.reshape(t, -1), x.reshape(t, d), b_w_out, g0, b0)
            x = x.reshape(bsz, seq_len, d)
        else:
            x = _mixer_layer(_pool_layer_body, j, x, c_w_in,
                             [_block_diag(c_pool_w[j]).astype(BF16), c_pool_scale[j].reshape(1, -1)], kt, v,
                             c_w_out, g0, b0)
        f = i // 2
        x2 = x.reshape(t, d)
        if i % 2 == 0:
            x2 = _ffn_ln(f, x2, *ffn_w, g1, b1)
        else:
            x2 = _moe_ln(f, x2, moe_router[f], *moe_w, g1, b1)
        x = x2.reshape(bsz, seq_len, d)
    return x
```
